```python
import jax, jax.numpy as jnp
from jax import lax
import numpy as np

D_MODEL = 1024
BATCH = 4
SEQ = 8192
DEPTH = 1

GRID_W = 64
CTX_LEN = 256
D_MIX = D_MODEL
A_W = D_MIX // 2
A_GROUPS = 8
A_HEAD = A_W // A_GROUPS
CHUNK = 128
B_W = D_MIX - A_W
B_HEADS = 8
B_HEAD = B_W // B_HEADS
DECAY_LORA = 64
AAA_LORA = 64
GATE_LORA = 128
CONV_K = 3
OFF_K = 0
OFF_V = B_W
OFF_R = 2 * B_W
OFF_WD = 3 * B_W
OFF_AD = OFF_WD + 2 * DECAY_LORA
OFF_GD = OFF_AD + 2 * AAA_LORA
OFF_U = OFF_GD + GATE_LORA
OFF_VA = OFF_U + A_W
D_IN = OFF_VA + A_W
N_GROUPS = 4
EXPERTS_PER_GROUP = 8
N_EXPERTS = N_GROUPS * EXPERTS_PER_GROUP
TOP_K_IN_GROUP = 2
D_EXPERT = 256
NORM_EPS = 1e-6
LN_EPS = 1e-5
GN_EPS = 64e-5

kernel_name = "hymba_gmlp_rwkv7_hmoe_dit_layer"


def rms_norm(x, g):
    x32 = x.astype(jnp.float32)
    y = x32 * lax.rsqrt(jnp.mean(x32 * x32, axis=-1, keepdims=True) + NORM_EPS)
    return (y * g.astype(jnp.float32)).astype(x.dtype)


def adaln(cond, w_mod, b_mod, n_chunks):
    m = jax.nn.silu(cond) @ w_mod[:, : n_chunks * D_MODEL] + b_mod[: n_chunks * D_MODEL]
    return jnp.split(m, n_chunks, axis=-1)


def modulate(h, shift, scale):
    return h * (1.0 + scale) + shift


def grid_dwconv(z, kernel, width):
    b, t, ch = z.shape
    rows = t // width
    img = z.reshape(b, rows, width, ch)
    out = lax.conv_general_dilated(
        img, kernel[:, :, None, :].astype(z.dtype), (1, 1), "SAME",
        dimension_numbers=("NHWC", "HWIO", "NHWC"), feature_group_count=ch)
    return out.reshape(b, t, ch)


def chunk_gmlp(u, v, ln_g, ln_b, w_s, b_s):
    b, t, _ = u.shape
    n_chunks = t // CHUNK
    uh = jax.nn.gelu(u).reshape(b, n_chunks, CHUNK, A_GROUPS, A_HEAD)
    v32 = jax.nn.gelu(v).astype(jnp.float32).reshape(b, n_chunks, CHUNK, A_GROUPS, A_HEAD)
    mu = jnp.mean(v32, axis=-1, keepdims=True)
    var = jnp.mean(jnp.square(v32 - mu), axis=-1, keepdims=True)
    vn = (v32 - mu) * lax.rsqrt(var + LN_EPS)
    vn = (vn * ln_g.reshape(A_GROUPS, A_HEAD) + ln_b.reshape(A_GROUPS, A_HEAD)).astype(u.dtype)
    s = jnp.einsum("gpq,bcqgd->bcpgd", w_s, vn) + b_s.T[:, :, None]
    return (uh * s).reshape(b, t, A_W)


def to_heads(z):
    return z.reshape(z.shape[0], z.shape[1], B_HEADS, B_HEAD)


def zero_state(b):
    return jnp.zeros((b, B_HEADS, B_HEAD, B_HEAD), jnp.float32)


def rwkv_prepare(k, lora, p):
    kk = to_heads(k * p["k_k"]).astype(jnp.float32)
    kk = kk * lax.rsqrt(jnp.maximum(jnp.sum(kk * kk, axis=-1, keepdims=True), 1e-24))
    dirs = []
    for d in range(2):
        lw = lora[..., d * DECAY_LORA:(d + 1) * DECAY_LORA]
        la = lora[..., 2 * DECAY_LORA + d * AAA_LORA: 2 * DECAY_LORA + (d + 1) * AAA_LORA]
        w = -jax.nn.softplus(-(p["decay_w0"][d] + jnp.tanh(lw) @ p["decay_up"][d])) - 0.5
        decay = jnp.exp(-jnp.exp(w.astype(jnp.float32)))
        a = jax.nn.sigmoid(p["iclr_a0"][d] + la @ p["iclr_up"][d])
        k_mod = k * (1.0 + (a - 1.0) * p["k_a"][d])
        dirs.append((to_heads(decay), to_heads(a), to_heads(k_mod)))
    return kk, dirs


def _state_update(S, w_t, kk_t, kka_t, k_t, v_t):
    s_kk = jnp.einsum("bhvk,bhk->bhv", S, kk_t)
    return (S * w_t[:, :, None, :] - s_kk[..., None] * kka_t[:, :, None, :]
            + v_t[..., None] * k_t[:, :, None, :])


def _time_major(a):
    return jnp.moveaxis(a.astype(jnp.float32), 1, 0)


def rwkv_scan(state0, decay, kk, kka, k, v, r, reverse):
    xs = (_time_major(decay), _time_major(kk), _time_major(kka), _time_major(k),
          _time_major(v), _time_major(r))

    def step(S, inp):
        S = _state_update(S, *inp[:5])
        return S, jnp.einsum("bhvk,bhk->bhv", S, inp[5])

    s_final, ys = lax.scan(step, state0, xs, reverse=reverse)
    return jnp.moveaxis(ys, 0, 1), s_final


def rwkv_final_state(state0, decay, kk, kka, k, v, reverse):
    xs = (_time_major(decay), _time_major(kk), _time_major(kka), _time_major(k), _time_major(v))

    def step(S, inp):
        return _state_update(S, *inp), None

    s_final, _ = lax.scan(step, state0, xs, reverse=reverse)
    return s_final


def rwkv_mix(k, v, r, lora, gd, p, states0):
    b, t, _ = k.shape
    kk, dirs = rwkv_prepare(k, lora, p)
    vh, rh = to_heads(v), to_heads(r)
    ys, bonus, finals = [], [], []
    for d, (decay, a, k_mod) in enumerate(dirs):
        y_d, s_d = rwkv_scan(states0[d], decay, kk, kk * a, k_mod, vh, rh, reverse=(d == 1))
        ys.append(y_d)
        finals.append(s_d)
        bonus.append(jnp.sum(rh * k_mod * p["r_k"], axis=-1, keepdims=True))
    y = ys[0] + ys[1]
    mu = jnp.mean(y, axis=-1, keepdims=True)
    var = jnp.mean(jnp.square(y - mu), axis=-1, keepdims=True)
    y = (y - mu) * lax.rsqrt(var + GN_EPS)
    y = y * p["lnx_g"].reshape(B_HEADS, B_HEAD) + p["lnx_b"].reshape(B_HEADS, B_HEAD)
    y = y.astype(k.dtype) + (bonus[0] + bonus[1]) * vh
    gate = jax.nn.sigmoid(gd) @ p["outgate_up"]
    return y.reshape(b, t, B_W) * gate, finals


def token_mixer(h, p, width, states0):
    z = h @ p["w_in"]
    rkv = grid_dwconv(z[..., OFF_K:OFF_WD], p["conv_rkv"], width)
    k, v, r = rkv[..., :B_W], rkv[..., B_W:2 * B_W], rkv[..., 2 * B_W:]
    out_b, finals = rwkv_mix(k, v, r, z[..., OFF_WD:OFF_GD], z[..., OFF_GD:OFF_U], p, states0)
    out_a = chunk_gmlp(z[..., OFF_U:OFF_VA], z[..., OFF_VA:], p["gm_ln_g"], p["gm_ln_b"],
                       p["w_spatial"], p["b_spatial"])
    return jnp.concatenate([out_a, out_b], axis=-1) @ p["w_out"], finals


def context_states(h_ctx, p):
    kv = grid_dwconv(h_ctx @ p["w_in"][:, OFF_K:OFF_R], p["conv_rkv"][..., OFF_K:OFF_R], CTX_LEN)
    k, v = kv[..., :B_W], kv[..., B_W:]
    kk, dirs = rwkv_prepare(k, h_ctx @ p["w_in"][:, OFF_WD:OFF_GD], p)
    s0 = zero_state(h_ctx.shape[0])
    vh = to_heads(v)
    return [rwkv_final_state(s0, decay, kk, kk * a, k_mod, vh, reverse=(d == 1))
            for d, (decay, a, k_mod) in enumerate(dirs)]


def hier_moe(h, p):
    shape = h.shape
    h2 = h.reshape(-1, D_MODEL)
    n = h2.shape[0]
    grp_logits = (h2 @ p["w_router_grp"] + p["b_router_grp"]).astype(jnp.float32)
    g_w, g_i = lax.top_k(jax.nn.softmax(grp_logits, axis=-1), 1)
    exp_logits = (h2 @ p["w_router_exp"] + p["b_router_exp"]).astype(jnp.float32)
    exp_logits = exp_logits.reshape(n, N_GROUPS, EXPERTS_PER_GROUP)
    sel = jnp.take_along_axis(exp_logits, g_i[:, :, None], axis=1)[:, 0]
    e_w, e_i = lax.top_k(jax.nn.softmax(sel, axis=-1), TOP_K_IN_GROUP)
    e_w = e_w / jnp.sum(e_w, axis=-1, keepdims=True)
    weights = g_w * e_w
    expert_id = g_i * EXPERTS_PER_GROUP + e_i
    combine = jnp.einsum("nke,nk->ne", jax.nn.one_hot(expert_id, N_EXPERTS, dtype=jnp.float32),
                         weights).astype(h.dtype)
    y = jnp.zeros_like(h2)
    for e in range(N_EXPERTS):
        hid = jax.nn.silu(h2 @ p["w_gate"][e]) * (h2 @ p["w_up"][e])
        y = y + combine[:, e:e + 1] * (hid @ p["w_down"][e])
    return y.reshape(shape)


def setup_inputs(seed: int = 0) -> dict:
    key = jax.random.key(seed)
    ks = iter(jax.random.split(key, 40))

    def nrm(shape, scale):
        return scale * jax.random.normal(next(ks), shape, jnp.float32)

    L, D = DEPTH, D_MODEL
    return {
        "x": nrm((BATCH, SEQ, D), 1.0),
        "c": nrm((BATCH, D), 1.0),
        "ctx": nrm((BATCH, CTX_LEN, D), 1.0),
        "c_ctx": nrm((D,), 1.0),
        "w_mod": nrm((L, D, 6 * D), 0.5 * D ** -0.5),
        "b_mod": nrm((L, 6 * D), 0.01),
        "g_pre1": 1.0 + nrm((L, D), 0.05),
        "g_post1": 1.0 + nrm((L, D), 0.05),
        "g_pre2": 1.0 + nrm((L, D), 0.05),
        "g_post2": 1.0 + nrm((L, D), 0.05),
        "w_in": nrm((L, D, D_IN), D ** -0.5),
        "conv_rkv": nrm((L, CONV_K, CONV_K, 3 * B_W), 1.0 / 3.0),
        "gm_ln_g": 1.0 + nrm((L, A_W), 0.05),
        "gm_ln_b": nrm((L, A_W), 0.01),
        "w_spatial": nrm((L, A_GROUPS, CHUNK, CHUNK), CHUNK ** -0.5),
        "b_spatial": 1.0 + nrm((L, A_GROUPS, CHUNK), 0.05),
        "decay_w0": jax.random.uniform(next(ks), (L, 2, B_W), jnp.float32, -6.0, 0.0),
        "decay_up": nrm((L, 2, DECAY_LORA, B_W), 0.5 * DECAY_LORA ** -0.5),
        "iclr_a0": nrm((L, 2, B_W), 0.5),
        "iclr_up": nrm((L, 2, AAA_LORA, B_W), 0.5 * AAA_LORA ** -0.5),
        "k_k": 1.0 + nrm((L, B_W), 0.1),
        "k_a": 1.0 + nrm((L, 2, B_W), 0.1),
        "r_k": nrm((L, B_HEADS, B_HEAD), 0.1),
        "outgate_up": nrm((L, GATE_LORA, B_W), GATE_LORA ** -0.5),
        "lnx_g": 1.0 + nrm((L, B_W), 0.05),
        "lnx_b": nrm((L, B_W), 0.01),
        "w_out": nrm((L, D_MIX, D), D_MIX ** -0.5),
        "w_router_grp": nrm((L, D, N_GROUPS), D ** -0.5),
        "b_router_grp": nrm((L, N_GROUPS), 0.01),
        "w_router_exp": nrm((L, D, N_EXPERTS), D ** -0.5),
        "b_router_exp": nrm((L, N_EXPERTS), 0.01),
        "w_gate": nrm((L, N_EXPERTS, D, D_EXPERT), D ** -0.5),
        "w_up": nrm((L, N_EXPERTS, D, D_EXPERT), D ** -0.5),
        "w_down": nrm((L, N_EXPERTS, D_EXPERT, D), D_EXPERT ** -0.5),
    }


def reference(x, c, ctx, c_ctx, w_mod, b_mod, g_pre1, g_post1, g_pre2, g_post2, w_in, conv_rkv,
              gm_ln_g, gm_ln_b, w_spatial, b_spatial, decay_w0, decay_up, iclr_a0, iclr_up,
              k_k, k_a, r_k, outgate_up, lnx_g, lnx_b, w_out, w_router_grp, b_router_grp,
              w_router_exp, b_router_exp, w_gate, w_up, w_down):
    for layer in range(DEPTH):
        p = {
            "w_in": w_in[layer], "conv_rkv": conv_rkv[layer],
            "gm_ln_g": gm_ln_g[layer], "gm_ln_b": gm_ln_b[layer],
            "w_spatial": w_spatial[layer], "b_spatial": b_spatial[layer],
            "decay_w0": decay_w0[layer], "decay_up": decay_up[layer],
            "iclr_a0": iclr_a0[layer], "iclr_up": iclr_up[layer],
            "k_k": k_k[layer], "k_a": k_a[layer], "r_k": r_k[layer],
            "outgate_up": outgate_up[layer], "lnx_g": lnx_g[layer], "lnx_b": lnx_b[layer],
            "w_out": w_out[layer],
            "w_router_grp": w_router_grp[layer], "b_router_grp": b_router_grp[layer],
            "w_router_exp": w_router_exp[layer], "b_router_exp": b_router_exp[layer],
            "w_gate": w_gate[layer], "w_up": w_up[layer], "w_down": w_down[layer],
        }
        last = layer == DEPTH - 1
        shift1, scale1, gate1, shift2, scale2, gate2 = adaln(c[:, None, :], w_mod[layer], b_mod[layer], 6)
        if last:
            cshift1, cscale1 = adaln(c_ctx, w_mod[layer], b_mod[layer], 2)
            h_ctx = modulate(rms_norm(ctx, g_pre1[layer]), cshift1, cscale1)
            states = context_states(h_ctx, p)
        else:
            cshift1, cscale1, cgate1, cshift2, cscale2, cgate2 = adaln(c_ctx, w_mod[layer], b_mod[layer], 6)
            h_ctx = modulate(rms_norm(ctx, g_pre1[layer]), cshift1, cscale1)
            zs = zero_state(ctx.shape[0])
            y_ctx, states = token_mixer(h_ctx, p, CTX_LEN, [zs, zs])
            ctx = ctx + cgate1 * rms_norm(y_ctx, g_post1[layer])
            h_ctx = modulate(rms_norm(ctx, g_pre2[layer]), cshift2, cscale2)
            ctx = ctx + cgate2 * rms_norm(hier_moe(h_ctx, p), g_post2[layer])
        h = modulate(rms_norm(x, g_pre1[layer]), shift1, scale1)
        y, _ = token_mixer(h, p, GRID_W, states)
        x = x + gate1 * rms_norm(y, g_post1[layer])
        h = modulate(rms_norm(x, g_pre2[layer]), shift2, scale2)
        x = x + gate2 * rms_norm(hier_moe(h, p), g_post2[layer])
    return x
```

```python
import functools
import math

import numpy as np
import jax
import jax.numpy as jnp
from jax import lax
from jax.experimental import pallas as pl
from jax.experimental.pallas import tpu as pltpu

D_MODEL = 1024
GRID_W = 64
A_W = 512
A_GROUPS = 8
GM_CHUNK = 128
B_W = 512
B_HEADS = 8
B_HEAD = 64
DECAY_LORA = 64
AAA_LORA = 64
GATE_LORA = 128
OFF_WD = 3 * B_W
OFF_GD = OFF_WD + 2 * DECAY_LORA + 2 * AAA_LORA
OFF_U = OFF_GD + GATE_LORA
OFF_VA = OFF_U + A_W
D_IN = OFF_VA + A_W
N_GROUPS = 4
EXPERTS_PER_GROUP = 8
N_EXPERTS = 32
D_EXPERT = 256
NORM_EPS = 1e-6
LN_EPS = 1e-5
GN_EPS = 64e-5

LANES = 128
RW_CHUNK = 128
N_PAIRS = B_W // LANES
VMEM_LIMIT = 48 * 1024 * 1024

BF16 = jnp.bfloat16
F32 = jnp.float32


def _dot(a, b):
    return jnp.dot(a, b, preferred_element_type=F32)


def _dot_nt(a, b):
    return lax.dot_general(a, b, (((1,), (1,)), ((), ())), preferred_element_type=F32)


def _split_dot(x, w_bf16):
    hi = x.astype(BF16)
    lo = (x - hi.astype(F32)).astype(BF16)
    return _dot(hi, w_bf16) + _dot(lo, w_bf16)


def _cparams(sem):
    return pltpu.CompilerParams(dimension_semantics=sem, vmem_limit_bytes=VMEM_LIMIT)


def _mod_kernel(c_ref, w_ref, b_ref, o_ref):
    c = c_ref[...]
    s = c * jax.nn.sigmoid(c)
    o_ref[...] = jnp.dot(s, w_ref[...], preferred_element_type=F32,
                         precision=lax.Precision.HIGHEST) + b_ref[...]


def _mod(cond8, w_mod, b_mod):
    n = w_mod.shape[1]
    tn = 1024
    return pl.pallas_call(
        _mod_kernel,
        out_shape=jax.ShapeDtypeStruct((8, n), F32),
        grid=(n // tn,),
        in_specs=[pl.BlockSpec((8, D_MODEL), lambda j: (0, 0)),
                  pl.BlockSpec((D_MODEL, tn), lambda j: (0, j)),
                  pl.BlockSpec((1, tn), lambda j: (0, j))],
        out_specs=pl.BlockSpec((8, tn), lambda j: (0, j)),
        compiler_params=_cparams(("arbitrary",)),
        name="mod",
    )(cond8, w_mod, b_mod.reshape(1, n))


def _inproj_kernel(x_ref, g_ref, sh_ref, sc_ref, w_ref, rkv_ref, lora_ref, gd_ref, u_ref, va_ref):
    x = x_ref[0]
    ms = jnp.mean(x * x, axis=-1, keepdims=True)
    h = x * lax.rsqrt(ms + NORM_EPS) * g_ref[...]
    h = h * (1.0 + sc_ref[0]) + sh_ref[0]
    hb = h.astype(BF16)
    rkv_ref[0] = _dot(hb, w_ref[:, 0:OFF_WD]).astype(rkv_ref.dtype)
    lora_ref[0] = _dot(hb, w_ref[:, OFF_WD:OFF_GD])
    gd_ref[0] = _dot(hb, w_ref[:, OFF_GD:OFF_U])
    u_ref[0] = _dot(hb, w_ref[:, OFF_U:OFF_VA]).astype(u_ref.dtype)
    va_ref[0] = _dot(hb, w_ref[:, OFF_VA:D_IN]).astype(va_ref.dtype)


def _inproj(x, g, shift, scale, w_in_bf16, tm):
    b, t, _ = x.shape
    tok = lambda width: pl.BlockSpec((1, tm, width), lambda bi, i: (bi, i, 0))
    vec = pl.BlockSpec((1, 1, D_MODEL), lambda bi, i: (bi, 0, 0))
    return pl.pallas_call(
        _inproj_kernel,
        out_shape=(jax.ShapeDtypeStruct((b, t, OFF_WD), BF16),
                   jax.ShapeDtypeStruct((b, t, OFF_GD - OFF_WD), F32),
                   jax.ShapeDtypeStruct((b, t, GATE_LORA), F32),
                   jax.ShapeDtypeStruct((b, t, A_W), BF16),
                   jax.ShapeDtypeStruct((b, t, A_W), BF16)),
        grid=(b, t // tm),
        in_specs=[tok(D_MODEL),
                  pl.BlockSpec((1, D_MODEL), lambda bi, i: (0, 0)),
                  vec, vec,
                  pl.BlockSpec((D_MODEL, D_IN), lambda bi, i: (0, 0))],
        out_specs=(tok(OFF_WD), tok(OFF_GD - OFF_WD), tok(GATE_LORA), tok(A_W), tok(A_W)),
        compiler_params=_cparams(("parallel", "parallel")),
        name="inproj",
    )(x, g.reshape(1, D_MODEL), shift, scale, w_in_bf16)


CONV_HALO = 128


def _conv_kernel(prev_ref, cur_ref, next_ref, w_ref, o_ref, e_ref, *, width, rows, tm):
    i = pl.program_id(1)
    cb = o_ref.shape[-1]
    e_ref[0:CONV_HALO, :] = prev_ref[0].astype(F32)
    e_ref[CONV_HALO:CONV_HALO + tm, :] = cur_ref[0].astype(F32)
    e_ref[CONV_HALO + tm:, :] = next_ref[0].astype(F32)
    tpos = i * tm + lax.broadcasted_iota(jnp.int32, (tm, cb), 0)
    col = tpos % width
    row = tpos // width
    col_ok = {-1: (col >= 1).astype(F32), 0: None, 1: (col <= width - 2).astype(F32)}
    row_ok = {-1: (row >= 1).astype(F32), 0: None, 1: (row <= rows - 2).astype(F32)}
    acc = jnp.zeros((tm, cb), F32)
    for dr in (-1, 0, 1):
        if rows == 1 and dr != 0:
            continue
        inner = jnp.zeros((tm, cb), F32)
        for dc in (-1, 0, 1):
            shift = dr * width + dc
            xs = e_ref[pl.ds(CONV_HALO + shift, tm), :]
            term = xs * w_ref[dr + 1, dc + 1:dc + 2, :]
            if col_ok[dc] is not None:
                term = term * col_ok[dc]
            inner = inner + term
        if row_ok[dr] is not None:
            inner = inner * row_ok[dr]
        acc = acc + inner
    o_ref[0] = acc


def _conv(z, w, width, tm, nch):
    b, t, _ = z.shape
    rows = t // width
    cb = 512
    hb = tm // CONV_HALO
    nhalo = t // CONV_HALO
    kern = functools.partial(_conv_kernel, width=width, rows=rows, tm=tm)
    return pl.pallas_call(
        kern,
        out_shape=jax.ShapeDtypeStruct((b, t, nch), F32),
        grid=(b, t // tm, nch // cb),
        in_specs=[pl.BlockSpec((1, CONV_HALO, cb), lambda bi, i, c: (bi, jnp.maximum(i * hb - 1, 0), c)),
                  pl.BlockSpec((1, tm, cb), lambda bi, i, c: (bi, i, c)),
                  pl.BlockSpec((1, CONV_HALO, cb), lambda bi, i, c: (bi, jnp.minimum((i + 1) * hb, nhalo - 1), c)),
                  pl.BlockSpec((3, 3, cb), lambda bi, i, c: (0, 0, c))],
        out_specs=pl.BlockSpec((1, tm, cb), lambda bi, i, c: (bi, i, c)),
        scratch_shapes=[pltpu.VMEM((tm + 2 * CONV_HALO, cb), F32)],
        compiler_params=_cparams(("parallel", "parallel", "parallel")),
        name="conv",
    )(z, z, z, w)


def _prep_kernel(k_ref, v_ref, r_ref, lora_ref, gd_ref, kk_w_ref, w0_ref, wdec_ref, a0_ref, wicl_ref,
                 ka_ref, rk_ref, wgate_ref, bd_ref, dmat_ref, smat_ref,
                 at_ref, bt_ref, kt_ref, rt_ref, vo_ref, vec_ref, bv_ref, gate_ref):
    k = k_ref[0]
    v = v_ref[0]
    r = r_ref[0]
    lora = lora_ref[0]
    bd = bd_ref[...]
    kk = k * kk_w_ref[...]
    ss = _split_dot(kk * kk, bd)
    kk = kk * lax.rsqrt(jnp.maximum(ss, 1e-24))
    tl = jnp.tanh(lora[:, 0:2 * DECAY_LORA]).astype(BF16)
    la = lora[:, 2 * DECAY_LORA:].astype(BF16)
    kmod_sum = jnp.zeros_like(k)
    for d in range(2):
        wraw = -jax.nn.softplus(-(w0_ref[d:d + 1, :] + _dot(tl, wdec_ref[d]))) - 0.5
        logw = -jnp.exp(wraw)
        a = jax.nn.sigmoid(a0_ref[d:d + 1, :] + _dot(la, wicl_ref[d]))
        kmod = k * (1.0 + (a - 1.0) * ka_ref[d:d + 1, :])
        kmod_sum = kmod_sum + kmod
        bvec = -(kk * a)
        rel = _split_dot_lhs(dmat_ref[d], logw)
        e_pos = jnp.exp(rel)
        e_neg = jnp.exp(-rel)
        at_ref[0, d] = (kk * jnp.exp(rel - logw)).astype(BF16)
        bt_ref[0, d] = (bvec * e_neg).astype(BF16)
        kt_ref[0, d] = (kmod * e_neg).astype(BF16)
        rt_ref[0, d] = (r * e_pos).astype(BF16)
        vec_ref[0, d, 0] = jnp.exp(_split_dot_lhs(smat_ref[d], logw))
    vo_ref[0] = v.astype(BF16)
    bonus = _split_dot(r * kmod_sum * rk_ref[...], bd)
    bv_ref[0] = bonus * v
    gate_ref[0] = _dot(jax.nn.sigmoid(gd_ref[0]).astype(BF16), wgate_ref[...])


def _split_dot_lhs(m_bf16, x):
    hi = x.astype(BF16)
    lo = (x - hi.astype(F32)).astype(BF16)
    return _dot(m_bf16, hi) + _dot(m_bf16, lo)


def _chunk_matrices(tm, chunk):
    t = np.arange(tm)
    same = (t[:, None] // chunk) == (t[None, :] // chunk)
    start = (t // chunk) * chunk
    half = chunk // 2
    d_fwd = same * ((t[None, :] <= t[:, None]).astype(np.float32)
                    - (t[None, :] < (start + half)[:, None]).astype(np.float32))
    d_rev = same * ((t[None, :] >= t[:, None]).astype(np.float32)
                    - (t[None, :] >= (start + half)[:, None]).astype(np.float32))
    nct = tm // chunk
    s_fwd = np.zeros((nct * 8, tm), np.float32)
    s_rev = np.zeros((nct * 8, tm), np.float32)
    for j in range(nct):
        in_chunk = (t // chunk) == j
        first = in_chunk & (t < j * chunk + half)
        second = in_chunk & (t >= j * chunk + half)
        s_fwd[8 * j + 0] = first
        s_fwd[8 * j + 1] = in_chunk
        s_fwd[8 * j + 2] = second
        s_rev[8 * j + 0] = second
        s_rev[8 * j + 1] = in_chunk
        s_rev[8 * j + 2] = first
    dmat = np.stack([d_fwd, d_rev]).astype(np.float32)
    smat = np.stack([s_fwd, s_rev])
    return jnp.asarray(dmat, BF16), jnp.asarray(smat, BF16)


def _prep(kvr, lora, gd, p, tm, chunk):
    b, t, _ = kvr.shape
    nct = tm // chunk
    dmat, smat = _chunk_matrices(tm, chunk)
    tokc = lambda c: pl.BlockSpec((1, tm, B_W), lambda bi, i: (bi, i, c))
    tok = lambda width: pl.BlockSpec((1, tm, width), lambda bi, i: (bi, i, 0))
    full = lambda shape: pl.BlockSpec(shape, lambda bi, i: (0,) * len(shape))
    dir_tok = pl.BlockSpec((1, 2, tm, B_W), lambda bi, i: (bi, 0, i, 0))
    dir_shape = jax.ShapeDtypeStruct((b, 2, t, B_W), BF16)
    outs = pl.pallas_call(
        _prep_kernel,
        out_shape=(dir_shape, dir_shape, dir_shape, dir_shape,
                   jax.ShapeDtypeStruct((b, t, B_W), BF16),
                   jax.ShapeDtypeStruct((b, 2, t // tm, nct * 8, B_W), F32),
                   jax.ShapeDtypeStruct((b, t, B_W), F32),
                   jax.ShapeDtypeStruct((b, t, B_W), F32)),
        grid=(b, t // tm),
        in_specs=[tokc(0), tokc(1), tokc(2), tok(OFF_GD - OFF_WD), tok(GATE_LORA),
                  full((1, B_W)), full((2, B_W)), full((2, 2 * DECAY_LORA, B_W)),
                  full((2, B_W)), full((2, 2 * AAA_LORA, B_W)), full((2, B_W)), full((1, B_W)),
                  full((GATE_LORA, B_W)), full((B_W, B_W)),
                  full((2, tm, tm)), full((2, nct * 8, tm))],
        out_specs=(dir_tok, dir_tok, dir_tok, dir_tok, tok(B_W),
                   pl.BlockSpec((1, 2, 1, nct * 8, B_W), lambda bi, i: (bi, 0, i, 0, 0)),
                   tok(B_W), tok(B_W)),
        compiler_params=_cparams(("parallel", "parallel")),
        name="prep",
    )(kvr, kvr, kvr, lora, gd, p["k_k"], p["decay_w0"], p["wdec"], p["iclr_a0"], p["wicl"],
      p["k_a"], p["r_k"], p["wgate"], p["bd"], dmat, smat)
    at, bt, kt, rt, vo, vec, bv, gate = outs
    vec = vec.reshape(b, 2, t // chunk, 8, B_W)
    return at, bt, kt, rt, vo, vec, bv, gate


def _rwkv_kernel(at_ref, bt_ref, kt_ref, rt_ref, v_ref, vec_ref, s0_ref, y_ref, sfin_ref, z_ref, *, chunk):
    d = pl.program_id(1)
    i = pl.program_id(2)
    L = chunk
    nstage = int(math.log2(L))

    @pl.when(i == 0)
    def _():
        z_ref[...] = s0_ref[0, 0]

    rowi = lax.broadcasted_iota(jnp.int32, (L, L), 0)
    coli = lax.broadcasted_iota(jnp.int32, (L, L), 1)
    diff = (rowi - coli) * (1 - 2 * d)
    strict = diff > 0
    incl = diff >= 0
    eye = rowi == coli
    eye_f = eye.astype(F32)
    lane = lax.broadcasted_iota(jnp.int32, (L, LANES), 1)
    head0 = lane < B_HEAD
    sub = lax.broadcasted_iota(jnp.int32, (LANES, LANES), 0)
    lan = lax.broadcasted_iota(jnp.int32, (LANES, LANES), 1)
    same_head = (sub < B_HEAD) == (lan < B_HEAD)
    eye_c = sub == lan

    def split_heads(x):
        zero = jnp.zeros_like(x)
        return jnp.where(head0, x, zero), jnp.where(head0, zero, x)

    def stack_heads(x):
        x0, x1 = split_heads(x)
        return jnp.concatenate([x0, x1], axis=0)

    for p in range(N_PAIRS):
        sl = slice(p * LANES, (p + 1) * LANES)
        a_t = at_ref[0, 0, :, sl]
        b_t = bt_ref[0, 0, :, sl]
        k_t = kt_ref[0, 0, :, sl]
        r_t = rt_ref[0, 0, :, sl]
        vv = v_ref[0, :, sl]
        em = vec_ref[0, 0, 0, 0:1, sl]
        pl_row = vec_ref[0, 0, 0, 1:2, sl]
        epl = vec_ref[0, 0, 0, 2:3, sl]
        z0 = z_ref[p]
        z0b = z0.astype(BF16)

        bk = jnp.concatenate([b_t, k_t], axis=0)
        a_h = split_heads(a_t)
        r_h = split_heads(r_t)
        t_h, mak_h, arb_h, ark_h = [], [], [], []
        for h in range(2):
            res = _dot_nt(jnp.concatenate([a_h[h], r_h[h]], axis=0), bk)
            mab = jnp.where(strict, res[0:L, 0:L], 0.0)
            mak_h.append(jnp.where(strict, res[0:L, L:2 * L], 0.0).astype(BF16))
            arb_h.append(jnp.where(incl, res[L:2 * L, 0:L], 0.0).astype(BF16))
            ark_h.append(jnp.where(incl, res[L:2 * L, L:2 * L], 0.0).astype(BF16))
            tt = eye_f + mab
            pw = mab
            for _ in range(nstage - 1):
                pwb = pw.astype(BF16)
                pw = _dot(pwb, pwb)
                tt = tt + _dot(tt.astype(BF16), pw.astype(BF16))
            t_h.append(tt.astype(BF16))

        v_stack = stack_heads(vv)
        mv = _dot(jnp.concatenate(mak_h, axis=1), v_stack)
        x_rhs = jnp.concatenate([jnp.concatenate(list(a_h), axis=0),
                                 stack_heads(mv.astype(BF16))], axis=1)
        wu = _dot(jnp.concatenate(t_h, axis=1), x_rhs)
        wa = wu[:, 0:LANES]
        uv = wu[:, LANES:2 * LANES]
        u = _dot((wa * em).astype(BF16), z0b) + uv
        ub = u.astype(BF16)
        y = (_dot((r_t.astype(F32) * em).astype(BF16), z0b)
             + _dot(jnp.concatenate(arb_h, axis=1), stack_heads(ub))
             + _dot(jnp.concatenate(ark_h, axis=1), v_stack))
        y_ref[0, 0, :, sl] = y

        bhat_t = jnp.transpose(b_t.astype(F32) * epl).astype(BF16)
        khat_t = jnp.transpose(k_t.astype(F32) * epl).astype(BF16)
        upd = _dot(jnp.concatenate([bhat_t, khat_t], axis=1), jnp.concatenate([ub, vv], axis=0))
        pl_col = jnp.sum(jnp.where(eye_c, pl_row, 0.0), axis=1, keepdims=True)
        z_ref[p] = pl_col * z0 + jnp.where(same_head, upd, 0.0)

    sfin_ref[0, 0] = z_ref[...]


def _rwkv(at, bt, kt, rt, vo, vec, s0, chunk):
    b, _, t, _ = at.shape
    nc = t // chunk

    def cidx(d, i):
        return jnp.where(d == 0, i, nc - 1 - i)

    dir_tok = pl.BlockSpec((1, 1, chunk, B_W), lambda bi, d, i: (bi, d, cidx(d, i), 0))
    state = pl.BlockSpec((1, 1, N_PAIRS, LANES, LANES), lambda bi, d, i: (bi, d, 0, 0, 0))
    kern = functools.partial(_rwkv_kernel, chunk=chunk)
    return pl.pallas_call(
        kern,
        out_shape=(jax.ShapeDtypeStruct((b, 2, t, B_W), F32),
                   jax.ShapeDtypeStruct((b, 2, N_PAIRS, LANES, LANES), F32)),
        grid=(b, 2, nc),
        in_specs=[dir_tok, dir_tok, dir_tok, dir_tok,
                  pl.BlockSpec((1, chunk, B_W), lambda bi, d, i: (bi, cidx(d, i), 0)),
                  pl.BlockSpec((1, 1, 1, 8, B_W), lambda bi, d, i: (bi, d, cidx(d, i), 0, 0)),
                  state],
        out_specs=(dir_tok, state),
        scratch_shapes=[pltpu.VMEM((N_PAIRS, LANES, LANES), F32)],
        compiler_params=_cparams(("parallel", "parallel", "arbitrary")),
        name="rwkv",
    )(at, bt, kt, rt, vo, vec, s0)


def _post_kernel(y_ref, bv_ref, gate_ref, u_ref, va_ref, x_ref, mod_ref,
                 lnxg_ref, lnxb_ref, gmg_ref, gmb_ref, wsp_ref, bsp_ref, bd_ref,
                 wout_ref, gpost_ref, gpre_ref, wr_ref, br_ref,
                 x1_ref, h2_ref, comb_ref, *, tm):
    bdm = bd_ref[...]
    inv = 1.0 / B_HEAD

    def group_norm(val, eps):
        mu = _split3_dot(val, bdm) * inv
        cen = val - mu
        var = _split3_dot(cen * cen, bdm) * inv
        return cen * lax.rsqrt(var + eps)

    y = y_ref[0, 0] + y_ref[0, 1]
    yb = group_norm(y, GN_EPS) * lnxg_ref[...] + lnxb_ref[...]
    out_b = (yb + bv_ref[0]) * gate_ref[0]

    uh = jax.nn.gelu(u_ref[0].astype(F32))
    vg = jax.nn.gelu(va_ref[0].astype(F32))
    vn = (group_norm(vg, LN_EPS) * gmg_ref[...] + gmb_ref[...]).astype(BF16)
    lane = lax.broadcasted_iota(jnp.int32, (GM_CHUNK, LANES), 1)
    head0 = lane < B_HEAD
    s_rows = []
    for c in range(tm // GM_CHUNK):
        cols = []
        for p in range(A_W // LANES):
            blk = vn[c * GM_CHUNK:(c + 1) * GM_CHUNK, p * LANES:(p + 1) * LANES]
            zero = jnp.zeros_like(blk)
            stack = jnp.concatenate([jnp.where(head0, blk, zero), jnp.where(head0, zero, blk)], axis=0)
            cols.append(_dot(wsp_ref[p], stack))
        s_rows.append(jnp.concatenate(cols, axis=1) + bsp_ref[...])
    s = jnp.concatenate(s_rows, axis=0)
    out_a = uh * s

    ymix = (_dot(out_a.astype(BF16), wout_ref[0:A_W, :]) + _dot(out_b.astype(BF16), wout_ref[A_W:, :]))
    gate1 = mod_ref[0, 0:1, :]
    shift2 = mod_ref[0, 1:2, :]
    scale2 = mod_ref[0, 2:3, :]
    ms = jnp.mean(ymix * ymix, axis=-1, keepdims=True)
    x1 = x_ref[0] + gate1 * (ymix * lax.rsqrt(ms + NORM_EPS) * gpost_ref[...])
    x1_ref[0] = x1
    ms2 = jnp.mean(x1 * x1, axis=-1, keepdims=True)
    h2 = x1 * lax.rsqrt(ms2 + NORM_EPS) * gpre_ref[...]
    h2 = h2 * (1.0 + scale2) + shift2
    h2_ref[0] = h2.astype(BF16)

    logits = jnp.dot(h2, wr_ref[...], preferred_element_type=F32,
                     precision=lax.Precision.HIGHEST) + br_ref[...]
    lni = lax.broadcasted_iota(jnp.int32, logits.shape, 1)
    ln = lni.astype(F32)
    lgrp = (lni // EXPERTS_PER_GROUP).astype(F32)
    neg = jnp.float32(-jnp.inf)
    big = jnp.float32(LANES)
    gmask = (lni >= N_EXPERTS) & (lni < N_EXPERTS + N_GROUPS)
    gl = jnp.where(gmask, logits, neg)
    gmax = jnp.max(gl, axis=-1, keepdims=True)
    gsum = jnp.sum(jnp.where(gmask, jnp.exp(gl - gmax), 0.0), axis=-1, keepdims=True)
    g_w = 1.0 / gsum
    g_i = jnp.min(jnp.where(gl == gmax, ln - N_EXPERTS, big), axis=-1, keepdims=True)
    emask = (lni < N_EXPERTS) & (lgrp == g_i)
    el = jnp.where(emask, logits, neg)
    l1 = jnp.max(el, axis=-1, keepdims=True)
    i1 = jnp.min(jnp.where(el == l1, ln, big), axis=-1, keepdims=True)
    el2 = jnp.where(ln == i1, neg, el)
    l2 = jnp.max(el2, axis=-1, keepdims=True)
    i2 = jnp.min(jnp.where(el2 == l2, ln, big), axis=-1, keepdims=True)
    e21 = jnp.exp(l2 - l1)
    w1 = 1.0 / (1.0 + e21)
    w2 = e21 * w1
    comb_ref[0] = g_w * (jnp.where(ln == i1, w1, 0.0) + jnp.where(ln == i2, w2, 0.0))


def _split3_dot(x, w_bf16):
    hi = x.astype(BF16)
    r1 = x - hi.astype(F32)
    mid = r1.astype(BF16)
    lo = (r1 - mid.astype(F32)).astype(BF16)
    return _dot(hi, w_bf16) + _dot(mid, w_bf16) + _dot(lo, w_bf16)


def _post(y, bv, gate, u, va, x, mod3, p, tm):
    b, t, _ = x.shape
    tok = lambda width: pl.BlockSpec((1, tm, width), lambda bi, i: (bi, i, 0))
    full = lambda shape: pl.BlockSpec(shape, lambda bi, i: (0,) * len(shape))
    kern = functools.partial(_post_kernel, tm=tm)
    return pl.pallas_call(
        kern,
        out_shape=(jax.ShapeDtypeStruct((b, t, D_MODEL), F32),
                   jax.ShapeDtypeStruct((b, t, D_MODEL), BF16),
                   jax.ShapeDtypeStruct((b, t, LANES), F32)),
        grid=(b, t // tm),
        in_specs=[pl.BlockSpec((1, 2, tm, B_W), lambda bi, i: (bi, 0, i, 0)),
                  tok(B_W), tok(B_W), tok(A_W), tok(A_W), tok(D_MODEL),
                  pl.BlockSpec((1, 8, D_MODEL), lambda bi, i: (bi, 0, 0)),
                  full((1, B_W)), full((1, B_W)), full((1, A_W)), full((1, A_W)),
                  full((A_W // LANES, GM_CHUNK, 2 * GM_CHUNK)), full((GM_CHUNK, A_W)), full((B_W, B_W)),
                  full((D_MODEL, D_MODEL)), full((1, D_MODEL)), full((1, D_MODEL)),
                  full((D_MODEL, LANES)), full((1, LANES))],
        out_specs=(tok(D_MODEL), tok(D_MODEL), tok(LANES)),
        compiler_params=_cparams(("parallel", "parallel")),
        name="post",
    )(y, bv, gate, u, va, x, mod3, p["lnx_g"], p["lnx_b"], p["gm_ln_g"], p["gm_ln_b"],
      p["wsp"], p["bsp"], p["bd"], p["w_out"], p["g_post1"], p["g_pre2"], p["w_router"], p["b_router"])


def _moe_kernel(h_ref, comb_ref, x1_ref, mod_ref, wg_ref, wu_ref, wd_ref, gpost_ref, o_ref, acc_ref):
    e = pl.program_id(2)

    @pl.when(e == 0)
    def _():
        acc_ref[...] = jnp.zeros_like(acc_ref)

    h = h_ref[0]
    comb = comb_ref[0]
    ln = lax.broadcasted_iota(jnp.int32, comb.shape, 1)
    ce = jnp.sum(jnp.where(ln == e, comb, 0.0), axis=-1, keepdims=True)
    gate = _dot(h, wg_ref[0])
    up = _dot(h, wu_ref[0])
    hid = gate * jax.nn.sigmoid(gate) * up
    acc_ref[...] += ce * _dot(hid.astype(BF16), wd_ref[0])

    @pl.when(e == N_EXPERTS - 1)
    def _():
        y = acc_ref[...]
        ms = jnp.mean(y * y, axis=-1, keepdims=True)
        o_ref[0] = x1_ref[0] + mod_ref[0, 3:4, :] * (y * lax.rsqrt(ms + NORM_EPS) * gpost_ref[...])


def _moe(h2, comb, x1, mod3, wg, wu, wd, g_post2, tm):
    b, t, _ = x1.shape
    tok = lambda width: pl.BlockSpec((1, tm, width), lambda bi, i, e: (bi, i, 0))
    return pl.pallas_call(
        _moe_kernel,
        out_shape=jax.ShapeDtypeStruct((b, t, D_MODEL), F32),
        grid=(b, t // tm, N_EXPERTS),
        in_specs=[tok(D_MODEL), tok(LANES), tok(D_MODEL),
                  pl.BlockSpec((1, 8, D_MODEL), lambda bi, i, e: (bi, 0, 0)),
                  pl.BlockSpec((1, D_MODEL, D_EXPERT), lambda bi, i, e: (e, 0, 0)),
                  pl.BlockSpec((1, D_MODEL, D_EXPERT), lambda bi, i, e: (e, 0, 0)),
                  pl.BlockSpec((1, D_EXPERT, D_MODEL), lambda bi, i, e: (e, 0, 0)),
                  pl.BlockSpec((1, D_MODEL), lambda bi, i, e: (0, 0))],
        out_specs=tok(D_MODEL),
        scratch_shapes=[pltpu.VMEM((tm, D_MODEL), F32)],
        compiler_params=_cparams(("parallel", "parallel", "arbitrary")),
        name="moe",
    )(h2, comb, x1, mod3, wg, wu, wd, g_post2.reshape(1, D_MODEL))


def _tile(t, pref):
    while t % pref:
        pref //= 2
    return pref


def kernel(x, c, ctx, c_ctx, w_mod, b_mod, g_pre1, g_post1, g_pre2, g_post2, w_in, conv_rkv, gm_ln_g, gm_ln_b, w_spatial, b_spatial, decay_w0, decay_up, iclr_a0, iclr_up, k_k, k_a, r_k, outgate_up, lnx_g, lnx_b, w_out, w_router_grp, b_router_grp, w_router_exp, b_router_exp, w_gate, w_up, w_down):
    assert w_mod.shape[0] == 1, "single-layer kernel"
    b, t, _ = x.shape
    tc = ctx.shape[1]
    assert b <= 7 and t % RW_CHUNK == 0 and tc % RW_CHUNK == 0 and t % GRID_W == 0

    head_of = np.arange(B_W) // B_HEAD
    bd = jnp.asarray(head_of[:, None] == head_of[None, :], BF16)
    zpad = jnp.zeros((2, DECAY_LORA, B_W), F32)
    wdec = jnp.stack([jnp.concatenate([decay_up[0, 0], zpad[0]], 0),
                      jnp.concatenate([zpad[0], decay_up[0, 1]], 0)]).astype(BF16)
    wicl = jnp.stack([jnp.concatenate([iclr_up[0, 0], zpad[0]], 0),
                      jnp.concatenate([zpad[0], iclr_up[0, 1]], 0)]).astype(BF16)
    wsp = w_spatial[0].reshape(A_W // LANES, 2, GM_CHUNK, GM_CHUNK)
    wsp = jnp.concatenate([wsp[:, 0], wsp[:, 1]], axis=-1).astype(BF16)
    bsp = jnp.repeat(b_spatial[0].T, A_W // A_GROUPS, axis=1)
    w_router = jnp.concatenate(
        [w_router_exp[0], w_router_grp[0],
         jnp.zeros((D_MODEL, LANES - N_EXPERTS - N_GROUPS), F32)], axis=1)
    b_router = jnp.concatenate(
        [b_router_exp[0], b_router_grp[0], jnp.zeros((LANES - N_EXPERTS - N_GROUPS,), F32)]).reshape(1, LANES)
    prm = {
        "k_k": k_k[0].reshape(1, B_W), "decay_w0": decay_w0[0], "wdec": wdec,
        "iclr_a0": iclr_a0[0], "wicl": wicl, "k_a": k_a[0], "r_k": r_k[0].reshape(1, B_W),
        "wgate": outgate_up[0].astype(BF16), "bd": bd,
        "lnx_g": lnx_g[0].reshape(1, B_W), "lnx_b": lnx_b[0].reshape(1, B_W),
        "gm_ln_g": gm_ln_g[0].reshape(1, A_W), "gm_ln_b": gm_ln_b[0].reshape(1, A_W),
        "wsp": wsp, "bsp": bsp, "w_out": w_out[0].astype(BF16),
        "g_post1": g_post1[0].reshape(1, D_MODEL), "g_pre2": g_pre2[0].reshape(1, D_MODEL),
        "w_router": w_router, "b_router": b_router,
    }
    w_in_b = w_in[0].astype(BF16)

    cond8 = jnp.concatenate([c, c_ctx[None, :], jnp.zeros((8 - b - 1, D_MODEL), F32)], axis=0)
    mod = _mod(cond8, w_mod[0], b_mod[0])
    mod6 = mod.reshape(8, 6, D_MODEL)
    shift1, scale1 = mod6[:b, 0:1], mod6[:b, 1:2]
    cshift1 = jnp.broadcast_to(mod6[b:b + 1, 0:1], (b, 1, D_MODEL))
    cscale1 = jnp.broadcast_to(mod6[b:b + 1, 1:2], (b, 1, D_MODEL))
    mod3 = jnp.concatenate([mod6[:b, 2:6], jnp.zeros((b, 4, D_MODEL), F32)], axis=1)

    c_rkv, c_lora, c_gd, _, _ = _inproj(ctx, g_pre1[0], cshift1, cscale1, w_in_b, _tile(tc, 256))
    c_kvr = _conv(c_rkv, conv_rkv[0], tc, _tile(tc, 256), 3 * B_W)
    c_at, c_bt, c_kt, c_rt, c_v, c_vec, _, _ = _prep(c_kvr, c_lora, c_gd, prm, _tile(tc, 256), RW_CHUNK)
    s_zero = jnp.zeros((b, 2, N_PAIRS, LANES, LANES), F32)
    _, states = _rwkv(c_at, c_bt, c_kt, c_rt, c_v, c_vec, s_zero, RW_CHUNK)

    rkv, lora, gd, u, va = _inproj(x, g_pre1[0], shift1, scale1, w_in_b, _tile(t, 512))
    kvr = _conv(rkv, conv_rkv[0], GRID_W, _tile(t, 512), 3 * B_W)
    at, bt, kt, rt, vo, vec, bv, gate = _prep(kvr, lora, gd, prm, _tile(t, 256), RW_CHUNK)
    y, _ = _rwkv(at, bt, kt, rt, vo, vec, states, RW_CHUNK)
    x1, h2, comb = _post(y, bv, gate, u, va, x, mod3, prm, _tile(t, 256))

    return _moe(h2, comb, x1, mod3, w_gate[0].astype(BF16), w_up[0].astype(BF16),
                w_down[0].astype(BF16), g_post2[0], _tile(t, 1024))
```

```python
import functools
import math

import numpy as np
import jax
import jax.numpy as jnp
from jax import lax
from jax.experimental import pallas as pl
from jax.experimental.pallas import tpu as pltpu

D_MODEL = 1024
GRID_W = 64
A_W = 512
A_GROUPS = 8
GM_CHUNK = 128
B_W = 512
B_HEADS = 8
B_HEAD = 64
DECAY_LORA = 64
AAA_LORA = 64
GATE_LORA = 128
OFF_WD = 3 * B_W
OFF_GD = OFF_WD + 2 * DECAY_LORA + 2 * AAA_LORA
OFF_U = OFF_GD + GATE_LORA
OFF_VA = OFF_U + A_W
D_IN = OFF_VA + A_W
N_GROUPS = 4
EXPERTS_PER_GROUP = 8
N_EXPERTS = 32
D_EXPERT = 256
NORM_EPS = 1e-6
LN_EPS = 1e-5
GN_EPS = 64e-5

LANES = 128
RW_CHUNK = 128
N_PAIRS = B_W // LANES
VMEM_LIMIT = 48 * 1024 * 1024

BF16 = jnp.bfloat16
F32 = jnp.float32


def _dot(a, b):
    return jnp.dot(a, b, preferred_element_type=F32)


def _dot_nt(a, b):
    return lax.dot_general(a, b, (((1,), (1,)), ((), ())), preferred_element_type=F32)


def _split_dot(x, w_bf16):
    hi = x.astype(BF16)
    lo = (x - hi.astype(F32)).astype(BF16)
    return _dot(hi, w_bf16) + _dot(lo, w_bf16)


def _cparams(sem):
    return pltpu.CompilerParams(dimension_semantics=sem, vmem_limit_bytes=VMEM_LIMIT)


def _mod_kernel(c_ref, w_ref, b_ref, o_ref):
    c = c_ref[...]
    s = c * jax.nn.sigmoid(c)
    o_ref[...] = jnp.dot(s, w_ref[...], preferred_element_type=F32,
                         precision=lax.Precision.HIGHEST) + b_ref[...]


def _mod(cond8, w_mod, b_mod):
    n = w_mod.shape[1]
    tn = 1024
    return pl.pallas_call(
        _mod_kernel,
        out_shape=jax.ShapeDtypeStruct((8, n), F32),
        grid=(n // tn,),
        in_specs=[pl.BlockSpec((8, D_MODEL), lambda j: (0, 0)),
                  pl.BlockSpec((D_MODEL, tn), lambda j: (0, j)),
                  pl.BlockSpec((1, tn), lambda j: (0, j))],
        out_specs=pl.BlockSpec((8, tn), lambda j: (0, j)),
        compiler_params=_cparams(("arbitrary",)),
        name="mod",
    )(cond8, w_mod, b_mod.reshape(1, n))


def _inproj_kernel(x_ref, g_ref, sh_ref, sc_ref, w_ref, rkv_ref, lora_ref, gd_ref, u_ref, va_ref):
    x = x_ref[0]
    ms = jnp.mean(x * x, axis=-1, keepdims=True)
    h = x * lax.rsqrt(ms + NORM_EPS) * g_ref[...]
    h = h * (1.0 + sc_ref[0]) + sh_ref[0]
    hb = h.astype(BF16)
    rkv_ref[0] = _dot(hb, w_ref[:, 0:OFF_WD]).astype(rkv_ref.dtype)
    lora_ref[0] = _dot(hb, w_ref[:, OFF_WD:OFF_GD])
    gd_ref[0] = _dot(hb, w_ref[:, OFF_GD:OFF_U])
    u_ref[0] = _dot(hb, w_ref[:, OFF_U:OFF_VA]).astype(u_ref.dtype)
    va_ref[0] = _dot(hb, w_ref[:, OFF_VA:D_IN]).astype(va_ref.dtype)


def _inproj(x, g, shift, scale, w_in_bf16, tm):
    b, t, _ = x.shape
    tok = lambda width: pl.BlockSpec((1, tm, width), lambda bi, i: (bi, i, 0))
    vec = pl.BlockSpec((1, 1, D_MODEL), lambda bi, i: (bi, 0, 0))
    return pl.pallas_call(
        _inproj_kernel,
        out_shape=(jax.ShapeDtypeStruct((b, t, OFF_WD), BF16),
                   jax.ShapeDtypeStruct((b, t, OFF_GD - OFF_WD), F32),
                   jax.ShapeDtypeStruct((b, t, GATE_LORA), F32),
                   jax.ShapeDtypeStruct((b, t, A_W), BF16),
                   jax.ShapeDtypeStruct((b, t, A_W), BF16)),
        grid=(b, t // tm),
        in_specs=[tok(D_MODEL),
                  pl.BlockSpec((1, D_MODEL), lambda bi, i: (0, 0)),
                  vec, vec,
                  pl.BlockSpec((D_MODEL, D_IN), lambda bi, i: (0, 0))],
        out_specs=(tok(OFF_WD), tok(OFF_GD - OFF_WD), tok(GATE_LORA), tok(A_W), tok(A_W)),
        compiler_params=_cparams(("parallel", "parallel")),
        name="inproj",
    )(x, g.reshape(1, D_MODEL), shift, scale, w_in_bf16)


CONV_HALO = 128


def _conv_kernel(prev_ref, cur_ref, next_ref, w_ref, o_ref, e_ref, *, width, rows, tm):
    i = pl.program_id(1)
    last = pl.num_programs(1) - 1
    cb = o_ref.shape[-1]
    win = tm + 2 * CONV_HALO
    window = jnp.concatenate([
        jnp.where(i > 0, prev_ref[0].astype(F32), 0.0),
        cur_ref[0].astype(F32),
        jnp.where(i < last, next_ref[0].astype(F32), 0.0)], axis=0)
    spos = i * tm - CONV_HALO + lax.broadcasted_iota(jnp.int32, (win, cb), 0)
    col = spos & (width - 1)
    e_ref[0] = jnp.where(col == width - 1, 0.0, window)
    e_ref[1] = window
    e_ref[2] = jnp.where(col == 0, 0.0, window)
    acc = None
    for dr in (-1, 0, 1):
        if rows == 1 and dr != 0:
            continue
        for dc in (-1, 0, 1):
            xs = e_ref[dc + 1, pl.ds(CONV_HALO + dr * width + dc, tm), :]
            term = xs * w_ref[dr + 1, dc + 1:dc + 2, :]
            acc = term if acc is None else acc + term
    o_ref[0] = acc


def _conv(z, w, width, tm, nch):
    b, t, _ = z.shape
    rows = t // width
    assert width & (width - 1) == 0 and (width + 1 <= CONV_HALO or rows == 1)
    cb = 512
    hb = tm // CONV_HALO
    nhalo = t // CONV_HALO
    kern = functools.partial(_conv_kernel, width=width, rows=rows, tm=tm)
    return pl.pallas_call(
        kern,
        out_shape=jax.ShapeDtypeStruct((b, t, nch), F32),
        grid=(b, t // tm, nch // cb),
        in_specs=[pl.BlockSpec((1, CONV_HALO, cb), lambda bi, i, c: (bi, jnp.maximum(i * hb - 1, 0), c)),
                  pl.BlockSpec((1, tm, cb), lambda bi, i, c: (bi, i, c)),
                  pl.BlockSpec((1, CONV_HALO, cb), lambda bi, i, c: (bi, jnp.minimum((i + 1) * hb, nhalo - 1), c)),
                  pl.BlockSpec((3, 3, cb), lambda bi, i, c: (0, 0, c))],
        out_specs=pl.BlockSpec((1, tm, cb), lambda bi, i, c: (bi, i, c)),
        scratch_shapes=[pltpu.VMEM((3, tm + 2 * CONV_HALO, cb), F32)],
        compiler_params=_cparams(("parallel", "parallel", "parallel")),
        name="conv",
    )(z, z, z, w)


def _prep_kernel(k_ref, v_ref, r_ref, lora_ref, gd_ref, kk_w_ref, w0_ref, wdec_ref, a0_ref, wicl_ref,
                 ka_ref, rk_ref, wgate_ref, bd_ref, dmat_ref, smat_ref,
                 at_ref, bt_ref, kt_ref, rt_ref, vo_ref, vec_ref, bv_ref, gate_ref):
    k = k_ref[0]
    v = v_ref[0]
    r = r_ref[0]
    lora = lora_ref[0]
    bd = bd_ref[...]
    kk = k * kk_w_ref[...]
    ss = _split_dot(kk * kk, bd)
    kk = kk * lax.rsqrt(jnp.maximum(ss, 1e-24))
    tl = jnp.tanh(lora[:, 0:2 * DECAY_LORA]).astype(BF16)
    la = lora[:, 2 * DECAY_LORA:].astype(BF16)
    kmod_sum = jnp.zeros_like(k)
    for d in range(2):
        wraw = -jax.nn.softplus(-(w0_ref[d:d + 1, :] + _dot(tl, wdec_ref[d]))) - 0.5
        logw = -jnp.exp(wraw)
        a = jax.nn.sigmoid(a0_ref[d:d + 1, :] + _dot(la, wicl_ref[d]))
        kmod = k * (1.0 + (a - 1.0) * ka_ref[d:d + 1, :])
        kmod_sum = kmod_sum + kmod
        bvec = -(kk * a)
        rel = _split_dot_lhs(dmat_ref[d], logw)
        e_pos = jnp.exp(rel)
        e_neg = jnp.exp(-rel)
        at_ref[0, d] = (kk * jnp.exp(rel - logw)).astype(BF16)
        bt_ref[0, d] = (bvec * e_neg).astype(BF16)
        kt_ref[0, d] = (kmod * e_neg).astype(BF16)
        rt_ref[0, d] = (r * e_pos).astype(BF16)
        vec_ref[0, d, 0] = jnp.exp(_split_dot_lhs(smat_ref[d], logw))
    vo_ref[0] = v.astype(BF16)
    bonus = _split_dot(r * kmod_sum * rk_ref[...], bd)
    bv_ref[0] = bonus * v
    gate_ref[0] = _dot(jax.nn.sigmoid(gd_ref[0]).astype(BF16), wgate_ref[...])


def _split_dot_lhs(m_bf16, x):
    hi = x.astype(BF16)
    lo = (x - hi.astype(F32)).astype(BF16)
    return _dot(m_bf16, hi) + _dot(m_bf16, lo)


def _chunk_matrices(tm, chunk):
    t = np.arange(tm)
    same = (t[:, None] // chunk) == (t[None, :] // chunk)
    start = (t // chunk) * chunk
    half = chunk // 2
    d_fwd = same * ((t[None, :] <= t[:, None]).astype(np.float32)
                    - (t[None, :] < (start + half)[:, None]).astype(np.float32))
    d_rev = same * ((t[None, :] >= t[:, None]).astype(np.float32)
                    - (t[None, :] >= (start + half)[:, None]).astype(np.float32))
    nct = tm // chunk
    s_fwd = np.zeros((nct * 8, tm), np.float32)
    s_rev = np.zeros((nct * 8, tm), np.float32)
    for j in range(nct):
        in_chunk = (t // chunk) == j
        first = in_chunk & (t < j * chunk + half)
        second = in_chunk & (t >= j * chunk + half)
        s_fwd[8 * j + 0] = first
        s_fwd[8 * j + 1] = in_chunk
        s_fwd[8 * j + 2] = second
        s_rev[8 * j + 0] = second
        s_rev[8 * j + 1] = in_chunk
        s_rev[8 * j + 2] = first
    dmat = np.stack([d_fwd, d_rev]).astype(np.float32)
    smat = np.stack([s_fwd, s_rev])
    return jnp.asarray(dmat, BF16), jnp.asarray(smat, BF16)


def _prep(kvr, lora, gd, p, tm, chunk):
    b, t, _ = kvr.shape
    nct = tm // chunk
    dmat, smat = _chunk_matrices(tm, chunk)
    tokc = lambda c: pl.BlockSpec((1, tm, B_W), lambda bi, i: (bi, i, c))
    tok = lambda width: pl.BlockSpec((1, tm, width), lambda bi, i: (bi, i, 0))
    full = lambda shape: pl.BlockSpec(shape, lambda bi, i: (0,) * len(shape))
    dir_tok = pl.BlockSpec((1, 2, tm, B_W), lambda bi, i: (bi, 0, i, 0))
    dir_shape = jax.ShapeDtypeStruct((b, 2, t, B_W), BF16)
    outs = pl.pallas_call(
        _prep_kernel,
        out_shape=(dir_shape, dir_shape, dir_shape, dir_shape,
                   jax.ShapeDtypeStruct((b, t, B_W), BF16),
                   jax.ShapeDtypeStruct((b, 2, t // tm, nct * 8, B_W), F32),
                   jax.ShapeDtypeStruct((b, t, B_W), F32),
                   jax.ShapeDtypeStruct((b, t, B_W), F32)),
        grid=(b, t // tm),
        in_specs=[tokc(0), tokc(1), tokc(2), tok(OFF_GD - OFF_WD), tok(GATE_LORA),
                  full((1, B_W)), full((2, B_W)), full((2, 2 * DECAY_LORA, B_W)),
                  full((2, B_W)), full((2, 2 * AAA_LORA, B_W)), full((2, B_W)), full((1, B_W)),
                  full((GATE_LORA, B_W)), full((B_W, B_W)),
                  full((2, tm, tm)), full((2, nct * 8, tm))],
        out_specs=(dir_tok, dir_tok, dir_tok, dir_tok, tok(B_W),
                   pl.BlockSpec((1, 2, 1, nct * 8, B_W), lambda bi, i: (bi, 0, i, 0, 0)),
                   tok(B_W), tok(B_W)),
        compiler_params=_cparams(("parallel", "parallel")),
        name="prep",
    )(kvr, kvr, kvr, lora, gd, p["k_k"], p["decay_w0"], p["wdec"], p["iclr_a0"], p["wicl"],
      p["k_a"], p["r_k"], p["wgate"], p["bd"], dmat, smat)
    at, bt, kt, rt, vo, vec, bv, gate = outs
    vec = vec.reshape(b, 2, t // chunk, 8, B_W)
    return at, bt, kt, rt, vo, vec, bv, gate


def _rwkv_kernel(at_ref, bt_ref, kt_ref, rt_ref, v_ref, vec_ref, s0_ref, y_ref, sfin_ref, z_ref, *, chunk):
    d = pl.program_id(1)
    i = pl.program_id(2)
    L = chunk
    assert L & (L - 1) == 0

    @pl.when(i == 0)
    def _():
        z_ref[...] = s0_ref[0, 0]

    rowi = lax.broadcasted_iota(jnp.int32, (L, L), 0)
    coli = lax.broadcasted_iota(jnp.int32, (L, L), 1)
    diff = (rowi - coli) * (1 - 2 * d)
    strict = diff > 0
    incl = diff >= 0
    eye = rowi == coli
    eye_f = eye.astype(F32)
    blk_xor = rowi ^ coli
    lane = lax.broadcasted_iota(jnp.int32, (L, LANES), 1)
    head0 = lane < B_HEAD
    sub = lax.broadcasted_iota(jnp.int32, (LANES, LANES), 0)
    lan = lax.broadcasted_iota(jnp.int32, (LANES, LANES), 1)
    same_head = (sub < B_HEAD) == (lan < B_HEAD)
    eye_c = sub == lan

    def split_heads(x):
        zero = jnp.zeros_like(x)
        return jnp.where(head0, x, zero), jnp.where(head0, zero, x)

    def stack_heads(x):
        x0, x1 = split_heads(x)
        return jnp.concatenate([x0, x1], axis=0)

    pairs = range(N_PAIRS)
    sls = [slice(p * LANES, (p + 1) * LANES) for p in pairs]
    a_t = [at_ref[0, 0, :, sl] for sl in sls]
    b_t = [bt_ref[0, 0, :, sl] for sl in sls]
    k_t = [kt_ref[0, 0, :, sl] for sl in sls]
    r_t = [rt_ref[0, 0, :, sl] for sl in sls]
    vv = [v_ref[0, :, sl] for sl in sls]
    em = [vec_ref[0, 0, 0, 0:1, sl] for sl in sls]
    pl_row = [vec_ref[0, 0, 0, 1:2, sl] for sl in sls]
    epl = [vec_ref[0, 0, 0, 2:3, sl] for sl in sls]
    z0 = [z_ref[p] for p in pairs]
    z0b = [z.astype(BF16) for z in z0]
    a_h = [split_heads(x) for x in a_t]
    r_h = [split_heads(x) for x in r_t]

    heads = [(p, h) for p in pairs for h in range(2)]
    tt, mb, mak, arb, ark = {}, {}, {}, {}, {}
    for p, h in heads:
        bk = jnp.concatenate([b_t[p], k_t[p]], axis=0)
        res = _dot_nt(jnp.concatenate([a_h[p][h], r_h[p][h]], axis=0), bk)
        mab = jnp.where(strict, res[0:L, 0:L], 0.0)
        mak[p, h] = jnp.where(strict, res[0:L, L:2 * L], 0.0).astype(BF16)
        arb[p, h] = jnp.where(incl, res[L:2 * L, 0:L], 0.0).astype(BF16)
        ark[p, h] = jnp.where(incl, res[L:2 * L, L:2 * L], 0.0).astype(BF16)
        tt[p, h] = eye_f + jnp.where(blk_xor == 1, mab, 0.0)
        mb[p, h] = mab.astype(BF16)
    s = 2
    while s < L:
        level = (blk_xor >= s) & (blk_xor < 2 * s)
        xx = {ph: _dot(mb[ph], tt[ph].astype(BF16)) for ph in heads}
        for ph in heads:
            w = _dot(tt[ph].astype(BF16), xx[ph].astype(BF16))
            tt[ph] = tt[ph] + jnp.where(level, w, 0.0)
        s *= 2

    v_stack = [stack_heads(x) for x in vv]
    mv = [_dot(jnp.concatenate([mak[p, 0], mak[p, 1]], axis=1), v_stack[p]) for p in pairs]
    wu = []
    for p in pairs:
        x_rhs = jnp.concatenate([jnp.concatenate(list(a_h[p]), axis=0),
                                 stack_heads(mv[p].astype(BF16))], axis=1)
        t_cat = jnp.concatenate([tt[p, 0].astype(BF16), tt[p, 1].astype(BF16)], axis=1)
        wu.append(_dot(t_cat, x_rhs))
    u = [_dot((wu[p][:, 0:LANES] * em[p]).astype(BF16), z0b[p]) + wu[p][:, LANES:2 * LANES] for p in pairs]
    ub = [x.astype(BF16) for x in u]
    for p in pairs:
        y = (_dot((r_t[p].astype(F32) * em[p]).astype(BF16), z0b[p])
             + _dot(jnp.concatenate([arb[p, 0], arb[p, 1]], axis=1), stack_heads(ub[p]))
             + _dot(jnp.concatenate([ark[p, 0], ark[p, 1]], axis=1), v_stack[p]))
        y_ref[0, 0, :, sls[p]] = y
    for p in pairs:
        bhat_t = jnp.transpose(b_t[p].astype(F32) * epl[p]).astype(BF16)
        khat_t = jnp.transpose(k_t[p].astype(F32) * epl[p]).astype(BF16)
        upd = _dot(jnp.concatenate([bhat_t, khat_t], axis=1), jnp.concatenate([ub[p], vv[p]], axis=0))
        pl_col = jnp.sum(jnp.where(eye_c, pl_row[p], 0.0), axis=1, keepdims=True)
        z_ref[p] = pl_col * z0[p] + jnp.where(same_head, upd, 0.0)

    sfin_ref[0, 0] = z_ref[...]


def _rwkv(at, bt, kt, rt, vo, vec, s0, chunk):
    b, _, t, _ = at.shape
    nc = t // chunk

    def cidx(d, i):
        return jnp.where(d == 0, i, nc - 1 - i)

    dir_tok = pl.BlockSpec((1, 1, chunk, B_W), lambda bi, d, i: (bi, d, cidx(d, i), 0))
    state = pl.BlockSpec((1, 1, N_PAIRS, LANES, LANES), lambda bi, d, i: (bi, d, 0, 0, 0))
    kern = functools.partial(_rwkv_kernel, chunk=chunk)
    return pl.pallas_call(
        kern,
        out_shape=(jax.ShapeDtypeStruct((b, 2, t, B_W), F32),
                   jax.ShapeDtypeStruct((b, 2, N_PAIRS, LANES, LANES), F32)),
        grid=(b, 2, nc),
        in_specs=[dir_tok, dir_tok, dir_tok, dir_tok,
                  pl.BlockSpec((1, chunk, B_W), lambda bi, d, i: (bi, cidx(d, i), 0)),
                  pl.BlockSpec((1, 1, 1, 8, B_W), lambda bi, d, i: (bi, d, cidx(d, i), 0, 0)),
                  state],
        out_specs=(dir_tok, state),
        scratch_shapes=[pltpu.VMEM((N_PAIRS, LANES, LANES), F32)],
        compiler_params=_cparams(("parallel", "parallel", "arbitrary")),
        name="rwkv",
    )(at, bt, kt, rt, vo, vec, s0)


def _post_kernel(y_ref, bv_ref, gate_ref, u_ref, va_ref, x_ref, mod_ref,
                 lnxg_ref, lnxb_ref, gmg_ref, gmb_ref, wsp_ref, bsp_ref, bd_ref,
                 wout_ref, gpost_ref, gpre_ref, wr_ref, br_ref,
                 x1_ref, h2_ref, comb_ref, *, tm):
    bdm = bd_ref[...]
    inv = 1.0 / B_HEAD

    def group_norm(val, eps):
        mu = _split_dot(val, bdm) * inv
        cen = val - mu
        var = _split_dot(cen * cen, bdm) * inv
        return cen * lax.rsqrt(var + eps)

    y = y_ref[0, 0] + y_ref[0, 1]
    yb = group_norm(y, GN_EPS) * lnxg_ref[...] + lnxb_ref[...]
    out_b = (yb + bv_ref[0]) * gate_ref[0]

    uh = jax.nn.gelu(u_ref[0].astype(F32))
    vg = jax.nn.gelu(va_ref[0].astype(F32))
    vn = (group_norm(vg, LN_EPS) * gmg_ref[...] + gmb_ref[...]).astype(BF16)
    lane = lax.broadcasted_iota(jnp.int32, (GM_CHUNK, LANES), 1)
    head0 = lane < B_HEAD
    s_rows = []
    for c in range(tm // GM_CHUNK):
        cols = []
        for p in range(A_W // LANES):
            blk = vn[c * GM_CHUNK:(c + 1) * GM_CHUNK, p * LANES:(p + 1) * LANES]
            zero = jnp.zeros_like(blk)
            stack = jnp.concatenate([jnp.where(head0, blk, zero), jnp.where(head0, zero, blk)], axis=0)
            cols.append(_dot(wsp_ref[p], stack))
        s_rows.append(jnp.concatenate(cols, axis=1) + bsp_ref[...])
    s = jnp.concatenate(s_rows, axis=0)
    out_a = uh * s

    ymix = (_dot(out_a.astype(BF16), wout_ref[0:A_W, :]) + _dot(out_b.astype(BF16), wout_ref[A_W:, :]))
    gate1 = mod_ref[0, 0:1, :]
    shift2 = mod_ref[0, 1:2, :]
    scale2 = mod_ref[0, 2:3, :]
    ms = jnp.mean(ymix * ymix, axis=-1, keepdims=True)
    x1 = x_ref[0] + gate1 * (ymix * lax.rsqrt(ms + NORM_EPS) * gpost_ref[...])
    x1_ref[0] = x1
    ms2 = jnp.mean(x1 * x1, axis=-1, keepdims=True)
    h2 = x1 * lax.rsqrt(ms2 + NORM_EPS) * gpre_ref[...]
    h2 = h2 * (1.0 + scale2) + shift2
    h2_ref[0] = h2.astype(BF16)

    h2_hi = h2.astype(BF16)
    h2_lo = (h2 - h2_hi.astype(F32)).astype(BF16)
    logits = (_dot(h2_hi, wr_ref[0]) + _dot(h2_lo, wr_ref[0]) + _dot(h2_hi, wr_ref[1])) + br_ref[...]
    lni = lax.broadcasted_iota(jnp.int32, logits.shape, 1)
    ln = lni.astype(F32)
    lgrp = (lni // EXPERTS_PER_GROUP).astype(F32)
    neg = jnp.float32(-jnp.inf)
    big = jnp.float32(LANES)
    gmask = (lni >= N_EXPERTS) & (lni < N_EXPERTS + N_GROUPS)
    gl = jnp.where(gmask, logits, neg)
    gmax = jnp.max(gl, axis=-1, keepdims=True)
    gsum = jnp.sum(jnp.where(gmask, jnp.exp(gl - gmax), 0.0), axis=-1, keepdims=True)
    g_w = 1.0 / gsum
    g_i = jnp.min(jnp.where(gl == gmax, ln - N_EXPERTS, big), axis=-1, keepdims=True)
    emask = (lni < N_EXPERTS) & (lgrp == g_i)
    el = jnp.where(emask, logits, neg)
    l1 = jnp.max(el, axis=-1, keepdims=True)
    i1 = jnp.min(jnp.where(el == l1, ln, big), axis=-1, keepdims=True)
    el2 = jnp.where(ln == i1, neg, el)
    l2 = jnp.max(el2, axis=-1, keepdims=True)
    i2 = jnp.min(jnp.where(el2 == l2, ln, big), axis=-1, keepdims=True)
    e21 = jnp.exp(l2 - l1)
    w1 = 1.0 / (1.0 + e21)
    w2 = e21 * w1
    comb_ref[0] = g_w * (jnp.where(ln == i1, w1, 0.0) + jnp.where(ln == i2, w2, 0.0))


def _post(y, bv, gate, u, va, x, mod3, p, tm):
    b, t, _ = x.shape
    tok = lambda width: pl.BlockSpec((1, tm, width), lambda bi, i: (bi, i, 0))
    full = lambda shape: pl.BlockSpec(shape, lambda bi, i: (0,) * len(shape))
    kern = functools.partial(_post_kernel, tm=tm)
    return pl.pallas_call(
        kern,
        out_shape=(jax.ShapeDtypeStruct((b, t, D_MODEL), F32),
                   jax.ShapeDtypeStruct((b, t, D_MODEL), BF16),
                   jax.ShapeDtypeStruct((b, t, LANES), F32)),
        grid=(b, t // tm),
        in_specs=[pl.BlockSpec((1, 2, tm, B_W), lambda bi, i: (bi, 0, i, 0)),
                  tok(B_W), tok(B_W), tok(A_W), tok(A_W), tok(D_MODEL),
                  pl.BlockSpec((1, 8, D_MODEL), lambda bi, i: (bi, 0, 0)),
                  full((1, B_W)), full((1, B_W)), full((1, A_W)), full((1, A_W)),
                  full((A_W // LANES, GM_CHUNK, 2 * GM_CHUNK)), full((GM_CHUNK, A_W)), full((B_W, B_W)),
                  full((D_MODEL, D_MODEL)), full((1, D_MODEL)), full((1, D_MODEL)),
                  full((2, D_MODEL, LANES)), full((1, LANES))],
        out_specs=(tok(D_MODEL), tok(D_MODEL), tok(LANES)),
        compiler_params=_cparams(("parallel", "parallel")),
        name="post",
    )(y, bv, gate, u, va, x, mod3, p["lnx_g"], p["lnx_b"], p["gm_ln_g"], p["gm_ln_b"],
      p["wsp"], p["bsp"], p["bd"], p["w_out"], p["g_post1"], p["g_pre2"], p["w_router"], p["b_router"])


MOE_EB = 4


def _moe_kernel(h_ref, comb_ref, x1_ref, mod_ref, wg_ref, wu_ref, wd_ref, gpost_ref, o_ref, acc_ref):
    g = pl.program_id(2)
    h = h_ref[0]
    comb = comb_ref[0]
    ln = lax.broadcasted_iota(jnp.int32, comb.shape, 1)
    gate = _dot(h, wg_ref[0])
    up = _dot(h, wu_ref[0])
    hid = gate * jax.nn.sigmoid(gate) * up
    parts = []
    for j in range(MOE_EB):
        ce = jnp.sum(jnp.where(ln == g * MOE_EB + j, comb, 0.0), axis=-1, keepdims=True)
        parts.append((hid[:, j * D_EXPERT:(j + 1) * D_EXPERT] * ce).astype(BF16))
    contrib = _dot(jnp.concatenate(parts, axis=1), wd_ref[0])

    @pl.when(g == 0)
    def _():
        acc_ref[...] = contrib

    @pl.when(g > 0)
    def _():
        acc_ref[...] += contrib

    @pl.when(g == pl.num_programs(2) - 1)
    def _():
        y = acc_ref[...]
        ms = jnp.mean(y * y, axis=-1, keepdims=True)
        o_ref[0] = x1_ref[0] + mod_ref[0, 3:4, :] * (y * lax.rsqrt(ms + NORM_EPS) * gpost_ref[...])


def _moe(h2, comb, x1, mod3, wg, wu, wd, g_post2, tm):
    b, t, _ = x1.shape
    ng = N_EXPERTS // MOE_EB
    wide = MOE_EB * D_EXPERT
    tok = lambda width: pl.BlockSpec((1, tm, width), lambda bi, i, e: (bi, i, 0))
    return pl.pallas_call(
        _moe_kernel,
        out_shape=jax.ShapeDtypeStruct((b, t, D_MODEL), F32),
        grid=(b, t // tm, ng),
        in_specs=[tok(D_MODEL), tok(LANES), tok(D_MODEL),
                  pl.BlockSpec((1, 8, D_MODEL), lambda bi, i, e: (bi, 0, 0)),
                  pl.BlockSpec((1, D_MODEL, wide), lambda bi, i, e: (e, 0, 0)),
                  pl.BlockSpec((1, D_MODEL, wide), lambda bi, i, e: (e, 0, 0)),
                  pl.BlockSpec((1, wide, D_MODEL), lambda bi, i, e: (e, 0, 0)),
                  pl.BlockSpec((1, D_MODEL), lambda bi, i, e: (0, 0))],
        out_specs=tok(D_MODEL),
        scratch_shapes=[pltpu.VMEM((tm, D_MODEL), F32)],
        compiler_params=_cparams(("parallel", "parallel", "arbitrary")),
        name="moe",
    )(h2, comb, x1, mod3, wg, wu, wd, g_post2.reshape(1, D_MODEL))


def _tile(t, pref):
    while t % pref:
        pref //= 2
    return pref


def kernel(x, c, ctx, c_ctx, w_mod, b_mod, g_pre1, g_post1, g_pre2, g_post2, w_in, conv_rkv, gm_ln_g, gm_ln_b, w_spatial, b_spatial, decay_w0, decay_up, iclr_a0, iclr_up, k_k, k_a, r_k, outgate_up, lnx_g, lnx_b, w_out, w_router_grp, b_router_grp, w_router_exp, b_router_exp, w_gate, w_up, w_down):
    assert w_mod.shape[0] == 1, "single-layer kernel"
    b, t, _ = x.shape
    tc = ctx.shape[1]
    assert b <= 7 and t % RW_CHUNK == 0 and tc % RW_CHUNK == 0 and t % GRID_W == 0

    head_of = np.arange(B_W) // B_HEAD
    bd = jnp.asarray(head_of[:, None] == head_of[None, :], BF16)
    zpad = jnp.zeros((2, DECAY_LORA, B_W), F32)
    wdec = jnp.stack([jnp.concatenate([decay_up[0, 0], zpad[0]], 0),
                      jnp.concatenate([zpad[0], decay_up[0, 1]], 0)]).astype(BF16)
    wicl = jnp.stack([jnp.concatenate([iclr_up[0, 0], zpad[0]], 0),
                      jnp.concatenate([zpad[0], iclr_up[0, 1]], 0)]).astype(BF16)
    wsp = w_spatial[0].reshape(A_W // LANES, 2, GM_CHUNK, GM_CHUNK)
    wsp = jnp.concatenate([wsp[:, 0], wsp[:, 1]], axis=-1).astype(BF16)
    bsp = jnp.repeat(b_spatial[0].T, A_W // A_GROUPS, axis=1)
    w_router = jnp.concatenate(
        [w_router_exp[0], w_router_grp[0],
         jnp.zeros((D_MODEL, LANES - N_EXPERTS - N_GROUPS), F32)], axis=1)
    w_router_hi = w_router.astype(BF16)
    w_router = jnp.stack([w_router_hi, (w_router - w_router_hi.astype(F32)).astype(BF16)])
    b_router = jnp.concatenate(
        [b_router_exp[0], b_router_grp[0], jnp.zeros((LANES - N_EXPERTS - N_GROUPS,), F32)]).reshape(1, LANES)
    prm = {
        "k_k": k_k[0].reshape(1, B_W), "decay_w0": decay_w0[0], "wdec": wdec,
        "iclr_a0": iclr_a0[0], "wicl": wicl, "k_a": k_a[0], "r_k": r_k[0].reshape(1, B_W),
        "wgate": outgate_up[0].astype(BF16), "bd": bd,
        "lnx_g": lnx_g[0].reshape(1, B_W), "lnx_b": lnx_b[0].reshape(1, B_W),
        "gm_ln_g": gm_ln_g[0].reshape(1, A_W), "gm_ln_b": gm_ln_b[0].reshape(1, A_W),
        "wsp": wsp, "bsp": bsp, "w_out": w_out[0].astype(BF16),
        "g_post1": g_post1[0].reshape(1, D_MODEL), "g_pre2": g_pre2[0].reshape(1, D_MODEL),
        "w_router": w_router, "b_router": b_router,
    }
    w_in_b = w_in[0].astype(BF16)

    cond8 = jnp.concatenate([c, c_ctx[None, :], jnp.zeros((8 - b - 1, D_MODEL), F32)], axis=0)
    mod = _mod(cond8, w_mod[0], b_mod[0])
    mod6 = mod.reshape(8, 6, D_MODEL)
    shift1, scale1 = mod6[:b, 0:1], mod6[:b, 1:2]
    cshift1 = jnp.broadcast_to(mod6[b:b + 1, 0:1], (b, 1, D_MODEL))
    cscale1 = jnp.broadcast_to(mod6[b:b + 1, 1:2], (b, 1, D_MODEL))
    mod3 = jnp.concatenate([mod6[:b, 2:6], jnp.zeros((b, 4, D_MODEL), F32)], axis=1)

    c_rkv, c_lora, c_gd, _, _ = _inproj(ctx, g_pre1[0], cshift1, cscale1, w_in_b, _tile(tc, 256))
    c_kvr = _conv(c_rkv, conv_rkv[0], tc, _tile(tc, 256), 3 * B_W)
    c_at, c_bt, c_kt, c_rt, c_v, c_vec, _, _ = _prep(c_kvr, c_lora, c_gd, prm, _tile(tc, 256), RW_CHUNK)
    s_zero = jnp.zeros((b, 2, N_PAIRS, LANES, LANES), F32)
    _, states = _rwkv(c_at, c_bt, c_kt, c_rt, c_v, c_vec, s_zero, RW_CHUNK)

    rkv, lora, gd, u, va = _inproj(x, g_pre1[0], shift1, scale1, w_in_b, _tile(t, 512))
    kvr = _conv(rkv, conv_rkv[0], GRID_W, _tile(t, 512), 3 * B_W)
    at, bt, kt, rt, vo, vec, bv, gate = _prep(kvr, lora, gd, prm, _tile(t, 256), RW_CHUNK)
    y, _ = _rwkv(at, bt, kt, rt, vo, vec, states, RW_CHUNK)
    x1, h2, comb = _post(y, bv, gate, u, va, x, mod3, prm, _tile(t, 256))

    ng = N_EXPERTS // MOE_EB

    def widen(w):
        return (w.astype(BF16).reshape(ng, MOE_EB, D_MODEL, D_EXPERT).transpose(0, 2, 1, 3)
                .reshape(ng, D_MODEL, MOE_EB * D_EXPERT))

    wd = w_down[0].astype(BF16).reshape(ng, MOE_EB * D_EXPERT, D_MODEL)
    return _moe(h2, comb, x1, mod3, widen(w_gate[0]), widen(w_up[0]), wd, g_post2[0], _tile(t, 1024))
```

```python
import functools
import math

import numpy as np
import jax
import jax.numpy as jnp
from jax import lax
from jax.experimental import pallas as pl
from jax.experimental.pallas import tpu as pltpu

D_MODEL = 1024
GRID_W = 64
A_W = 512
A_GROUPS = 8
GM_CHUNK = 128
B_W = 512
B_HEADS = 8
B_HEAD = 64
DECAY_LORA = 64
AAA_LORA = 64
GATE_LORA = 128
OFF_WD = 3 * B_W
OFF_GD = OFF_WD + 2 * DECAY_LORA + 2 * AAA_LORA
OFF_U = OFF_GD + GATE_LORA
OFF_VA = OFF_U + A_W
D_IN = OFF_VA + A_W
N_GROUPS = 4
EXPERTS_PER_GROUP = 8
N_EXPERTS = 32
D_EXPERT = 256
NORM_EPS = 1e-6
LN_EPS = 1e-5
GN_EPS = 64e-5

LANES = 128
RW_CHUNK = 128
N_PAIRS = B_W // LANES
VMEM_LIMIT = 48 * 1024 * 1024

BF16 = jnp.bfloat16
F32 = jnp.float32


def _dot(a, b):
    return jnp.dot(a, b, preferred_element_type=F32)


def _dot_nt(a, b):
    return lax.dot_general(a, b, (((1,), (1,)), ((), ())), preferred_element_type=F32)


def _split_dot(x, w_bf16):
    hi = x.astype(BF16)
    lo = (x - hi.astype(F32)).astype(BF16)
    return _dot(hi, w_bf16) + _dot(lo, w_bf16)


def _cparams(sem):
    return pltpu.CompilerParams(dimension_semantics=sem, vmem_limit_bytes=VMEM_LIMIT)


def _mod_kernel(c_ref, w_ref, b_ref, o_ref):
    c = c_ref[...]
    s = c * jax.nn.sigmoid(c)
    o_ref[...] = jnp.dot(s, w_ref[...], preferred_element_type=F32,
                         precision=lax.Precision.HIGHEST) + b_ref[...]


def _mod(cond8, w_mod, b_mod):
    n = w_mod.shape[1]
    tn = 1024
    return pl.pallas_call(
        _mod_kernel,
        out_shape=jax.ShapeDtypeStruct((8, n), F32),
        grid=(n // tn,),
        in_specs=[pl.BlockSpec((8, D_MODEL), lambda j: (0, 0)),
                  pl.BlockSpec((D_MODEL, tn), lambda j: (0, j)),
                  pl.BlockSpec((1, tn), lambda j: (0, j))],
        out_specs=pl.BlockSpec((8, tn), lambda j: (0, j)),
        compiler_params=_cparams(("arbitrary",)),
        name="mod",
    )(cond8, w_mod, b_mod.reshape(1, n))


def _inproj_kernel(x_ref, g_ref, sh_ref, sc_ref, w_ref, rkv_ref, lora_ref, gd_ref, u_ref, va_ref):
    x = x_ref[0]
    ms = jnp.mean(x * x, axis=-1, keepdims=True)
    h = x * lax.rsqrt(ms + NORM_EPS) * g_ref[...]
    h = h * (1.0 + sc_ref[0]) + sh_ref[0]
    hb = h.astype(BF16)
    rkv_ref[0] = _dot(hb, w_ref[:, 0:OFF_WD]).astype(rkv_ref.dtype)
    lora_ref[0] = _dot(hb, w_ref[:, OFF_WD:OFF_GD])
    gd_ref[0] = _dot(hb, w_ref[:, OFF_GD:OFF_U])
    u_ref[0] = _dot(hb, w_ref[:, OFF_U:OFF_VA]).astype(u_ref.dtype)
    va_ref[0] = _dot(hb, w_ref[:, OFF_VA:D_IN]).astype(va_ref.dtype)


def _inproj(x, g, shift, scale, w_in_bf16, tm):
    b, t, _ = x.shape
    tok = lambda width: pl.BlockSpec((1, tm, width), lambda bi, i: (bi, i, 0))
    vec = pl.BlockSpec((1, 1, D_MODEL), lambda bi, i: (bi, 0, 0))
    return pl.pallas_call(
        _inproj_kernel,
        out_shape=(jax.ShapeDtypeStruct((b, t, OFF_WD), BF16),
                   jax.ShapeDtypeStruct((b, t, OFF_GD - OFF_WD), F32),
                   jax.ShapeDtypeStruct((b, t, GATE_LORA), F32),
                   jax.ShapeDtypeStruct((b, t, A_W), BF16),
                   jax.ShapeDtypeStruct((b, t, A_W), BF16)),
        grid=(b, t // tm),
        in_specs=[tok(D_MODEL),
                  pl.BlockSpec((1, D_MODEL), lambda bi, i: (0, 0)),
                  vec, vec,
                  pl.BlockSpec((D_MODEL, D_IN), lambda bi, i: (0, 0))],
        out_specs=(tok(OFF_WD), tok(OFF_GD - OFF_WD), tok(GATE_LORA), tok(A_W), tok(A_W)),
        compiler_params=_cparams(("parallel", "parallel")),
        name="inproj",
    )(x, g.reshape(1, D_MODEL), shift, scale, w_in_bf16)


CONV_HALO = 128


def _conv_kernel(prev_ref, cur_ref, next_ref, w_ref, o_ref, e_ref, *, width, rows, tm):
    i = pl.program_id(1)
    last = pl.num_programs(1) - 1
    cb = o_ref.shape[-1]
    win = tm + 2 * CONV_HALO
    window = jnp.concatenate([
        jnp.where(i > 0, prev_ref[0].astype(F32), 0.0),
        cur_ref[0].astype(F32),
        jnp.where(i < last, next_ref[0].astype(F32), 0.0)], axis=0)
    spos = i * tm - CONV_HALO + lax.broadcasted_iota(jnp.int32, (win, cb), 0)
    col = spos & (width - 1)
    e_ref[0] = jnp.where(col == width - 1, 0.0, window)
    e_ref[1] = window
    e_ref[2] = jnp.where(col == 0, 0.0, window)
    acc = None
    for dr in (-1, 0, 1):
        if rows == 1 and dr != 0:
            continue
        for dc in (-1, 0, 1):
            xs = e_ref[dc + 1, pl.ds(CONV_HALO + dr * width + dc, tm), :]
            term = xs * w_ref[dr + 1, dc + 1:dc + 2, :]
            acc = term if acc is None else acc + term
    o_ref[0] = acc


def _conv(z, w, width, tm, nch):
    b, t, _ = z.shape
    rows = t // width
    assert width & (width - 1) == 0 and (width + 1 <= CONV_HALO or rows == 1)
    cb = 512
    hb = tm // CONV_HALO
    nhalo = t // CONV_HALO
    kern = functools.partial(_conv_kernel, width=width, rows=rows, tm=tm)
    return pl.pallas_call(
        kern,
        out_shape=jax.ShapeDtypeStruct((b, t, nch), F32),
        grid=(b, t // tm, nch // cb),
        in_specs=[pl.BlockSpec((1, CONV_HALO, cb), lambda bi, i, c: (bi, jnp.maximum(i * hb - 1, 0), c)),
                  pl.BlockSpec((1, tm, cb), lambda bi, i, c: (bi, i, c)),
                  pl.BlockSpec((1, CONV_HALO, cb), lambda bi, i, c: (bi, jnp.minimum((i + 1) * hb, nhalo - 1), c)),
                  pl.BlockSpec((3, 3, cb), lambda bi, i, c: (0, 0, c))],
        out_specs=pl.BlockSpec((1, tm, cb), lambda bi, i, c: (bi, i, c)),
        scratch_shapes=[pltpu.VMEM((3, tm + 2 * CONV_HALO, cb), F32)],
        compiler_params=_cparams(("parallel", "parallel", "parallel")),
        name="conv",
    )(z, z, z, w)


def _prep_kernel(k_ref, v_ref, r_ref, lora_ref, gd_ref, kk_w_ref, w0_ref, wdec_ref, a0_ref, wicl_ref,
                 ka_ref, rk_ref, wgate_ref, bd_ref, dmat_ref, smat_ref,
                 at_ref, bt_ref, kt_ref, rt_ref, vo_ref, vec_ref, bv_ref, gate_ref):
    k = k_ref[0]
    v = v_ref[0]
    r = r_ref[0]
    lora = lora_ref[0]
    bd = bd_ref[...]
    kk = k * kk_w_ref[...]
    ss = _split_dot(kk * kk, bd)
    kk = kk * lax.rsqrt(jnp.maximum(ss, 1e-24))
    tl = jnp.tanh(lora[:, 0:2 * DECAY_LORA]).astype(BF16)
    la = lora[:, 2 * DECAY_LORA:].astype(BF16)
    kmod_sum = jnp.zeros_like(k)
    for d in range(2):
        logw = -math.exp(-0.5) * jax.nn.sigmoid(w0_ref[d:d + 1, :] + _dot(tl, wdec_ref[d]))
        a = jax.nn.sigmoid(a0_ref[d:d + 1, :] + _dot(la, wicl_ref[d]))
        kmod = k * (1.0 + (a - 1.0) * ka_ref[d:d + 1, :])
        kmod_sum = kmod_sum + kmod
        bvec = -(kk * a)
        rel = _split_dot_lhs(dmat_ref[d], logw)
        e_pos = jnp.exp(rel)
        e_neg = jnp.exp(-rel)
        at_ref[0, d] = (kk * jnp.exp(rel - logw)).astype(BF16)
        bt_ref[0, d] = (bvec * e_neg).astype(BF16)
        kt_ref[0, d] = (kmod * e_neg).astype(BF16)
        rt_ref[0, d] = (r * e_pos).astype(BF16)
        vec_ref[0, d, 0] = jnp.exp(_split_dot_lhs(smat_ref[d], logw))
    vo_ref[0] = v.astype(BF16)
    bonus = _split_dot(r * kmod_sum * rk_ref[...], bd)
    bv_ref[0] = bonus * v
    gate_ref[0] = _dot(jax.nn.sigmoid(gd_ref[0]).astype(BF16), wgate_ref[...])


def _split_dot_lhs(m_bf16, x):
    hi = x.astype(BF16)
    lo = (x - hi.astype(F32)).astype(BF16)
    return _dot(m_bf16, hi) + _dot(m_bf16, lo)


def _chunk_matrices(tm, chunk):
    t = np.arange(tm)
    same = (t[:, None] // chunk) == (t[None, :] // chunk)
    start = (t // chunk) * chunk
    half = chunk // 2
    d_fwd = same * ((t[None, :] <= t[:, None]).astype(np.float32)
                    - (t[None, :] < (start + half)[:, None]).astype(np.float32))
    d_rev = same * ((t[None, :] >= t[:, None]).astype(np.float32)
                    - (t[None, :] >= (start + half)[:, None]).astype(np.float32))
    nct = tm // chunk
    s_fwd = np.zeros((nct * 8, tm), np.float32)
    s_rev = np.zeros((nct * 8, tm), np.float32)
    for j in range(nct):
        in_chunk = (t // chunk) == j
        first = in_chunk & (t < j * chunk + half)
        second = in_chunk & (t >= j * chunk + half)
        s_fwd[8 * j + 0] = first
        s_fwd[8 * j + 1] = in_chunk
        s_fwd[8 * j + 2] = second
        s_rev[8 * j + 0] = second
        s_rev[8 * j + 1] = in_chunk
        s_rev[8 * j + 2] = first
    dmat = np.stack([d_fwd, d_rev]).astype(np.float32)
    smat = np.stack([s_fwd, s_rev])
    return jnp.asarray(dmat, BF16), jnp.asarray(smat, BF16)


def _prep(kvr, lora, gd, p, tm, chunk):
    b, t, _ = kvr.shape
    nct = tm // chunk
    dmat, smat = _chunk_matrices(tm, chunk)
    tokc = lambda c: pl.BlockSpec((1, tm, B_W), lambda bi, i: (bi, i, c))
    tok = lambda width: pl.BlockSpec((1, tm, width), lambda bi, i: (bi, i, 0))
    full = lambda shape: pl.BlockSpec(shape, lambda bi, i: (0,) * len(shape))
    dir_tok = pl.BlockSpec((1, 2, tm, B_W), lambda bi, i: (bi, 0, i, 0))
    dir_shape = jax.ShapeDtypeStruct((b, 2, t, B_W), BF16)
    outs = pl.pallas_call(
        _prep_kernel,
        out_shape=(dir_shape, dir_shape, dir_shape, dir_shape,
                   jax.ShapeDtypeStruct((b, t, B_W), BF16),
                   jax.ShapeDtypeStruct((b, 2, t // tm, nct * 8, B_W), F32),
                   jax.ShapeDtypeStruct((b, t, B_W), F32),
                   jax.ShapeDtypeStruct((b, t, B_W), F32)),
        grid=(b, t // tm),
        in_specs=[tokc(0), tokc(1), tokc(2), tok(OFF_GD - OFF_WD), tok(GATE_LORA),
                  full((1, B_W)), full((2, B_W)), full((2, 2 * DECAY_LORA, B_W)),
                  full((2, B_W)), full((2, 2 * AAA_LORA, B_W)), full((2, B_W)), full((1, B_W)),
                  full((GATE_LORA, B_W)), full((B_W, B_W)),
                  full((2, tm, tm)), full((2, nct * 8, tm))],
        out_specs=(dir_tok, dir_tok, dir_tok, dir_tok, tok(B_W),
                   pl.BlockSpec((1, 2, 1, nct * 8, B_W), lambda bi, i: (bi, 0, i, 0, 0)),
                   tok(B_W), tok(B_W)),
        compiler_params=_cparams(("parallel", "parallel")),
        name="prep",
    )(kvr, kvr, kvr, lora, gd, p["k_k"], p["decay_w0"], p["wdec"], p["iclr_a0"], p["wicl"],
      p["k_a"], p["r_k"], p["wgate"], p["bd"], dmat, smat)
    at, bt, kt, rt, vo, vec, bv, gate = outs
    vec = vec.reshape(b, 2, t // chunk, 8, B_W)
    return at, bt, kt, rt, vo, vec, bv, gate


def _rwkv_kernel(at0, bt0, kt0, rt0, v0, vec0, at1, bt1, kt1, rt1, v1, vec1, s0_ref,
                 y0_ref, y1_ref, sfin_ref, z_ref, *, chunk):
    i = pl.program_id(1)
    L = chunk
    assert L & (L - 1) == 0
    dirs = (0, 1)
    a_refs, b_refs, k_refs, r_refs = (at0, at1), (bt0, bt1), (kt0, kt1), (rt0, rt1)
    v_refs, vec_refs, y_refs = (v0, v1), (vec0, vec1), (y0_ref, y1_ref)

    @pl.when(i == 0)
    def _():
        z_ref[...] = s0_ref[0]

    rowi = lax.broadcasted_iota(jnp.int32, (L, L), 0)
    coli = lax.broadcasted_iota(jnp.int32, (L, L), 1)
    strict = (rowi > coli, rowi < coli)
    incl = (rowi >= coli, rowi <= coli)
    eye_f = (rowi == coli).astype(F32)
    blk_xor = rowi ^ coli
    lane = lax.broadcasted_iota(jnp.int32, (L, LANES), 1)
    head0 = lane < B_HEAD
    sub = lax.broadcasted_iota(jnp.int32, (LANES, LANES), 0)
    lan = lax.broadcasted_iota(jnp.int32, (LANES, LANES), 1)
    same_head = (sub < B_HEAD) == (lan < B_HEAD)
    eye_c = sub == lan

    def split_heads(x):
        zero = jnp.zeros_like(x)
        return jnp.where(head0, x, zero), jnp.where(head0, zero, x)

    def stack_heads(x):
        x0, x1 = split_heads(x)
        return jnp.concatenate([x0, x1], axis=0)

    pairs = range(N_PAIRS)
    sls = [slice(p * LANES, (p + 1) * LANES) for p in pairs]
    dps = [(d, p) for d in dirs for p in pairs]
    a_t = {(d, p): a_refs[d][0, 0, :, sls[p]] for d, p in dps}
    b_t = {(d, p): b_refs[d][0, 0, :, sls[p]] for d, p in dps}
    k_t = {(d, p): k_refs[d][0, 0, :, sls[p]] for d, p in dps}
    r_t = {(d, p): r_refs[d][0, 0, :, sls[p]] for d, p in dps}
    vv = {(d, p): v_refs[d][0, :, sls[p]] for d, p in dps}
    em = {(d, p): vec_refs[d][0, 0, 0, 0:1, sls[p]] for d, p in dps}
    pl_row = {(d, p): vec_refs[d][0, 0, 0, 1:2, sls[p]] for d, p in dps}
    epl = {(d, p): vec_refs[d][0, 0, 0, 2:3, sls[p]] for d, p in dps}
    z0 = {(d, p): z_ref[d, p] for d, p in dps}
    z0b = {dp: z0[dp].astype(BF16) for dp in dps}
    a_h = {dp: split_heads(a_t[dp]) for dp in dps}
    r_h = {dp: split_heads(r_t[dp]) for dp in dps}

    heads = [(d, p, h) for d, p in dps for h in range(2)]
    tt, mb, mak, arb, ark = {}, {}, {}, {}, {}
    for d, p, h in heads:
        bk = jnp.concatenate([b_t[d, p], k_t[d, p]], axis=0)
        res = _dot_nt(jnp.concatenate([a_h[d, p][h], r_h[d, p][h]], axis=0), bk)
        mab = jnp.where(strict[d], res[0:L, 0:L], 0.0)
        mak[d, p, h] = jnp.where(strict[d], res[0:L, L:2 * L], 0.0).astype(BF16)
        arb[d, p, h] = jnp.where(incl[d], res[L:2 * L, 0:L], 0.0).astype(BF16)
        ark[d, p, h] = jnp.where(incl[d], res[L:2 * L, L:2 * L], 0.0).astype(BF16)
        tt[d, p, h] = eye_f + jnp.where(blk_xor == 1, mab, 0.0)
        mb[d, p, h] = mab.astype(BF16)
    s = 2
    while s < L:
        level = (blk_xor >= s) & (blk_xor < 2 * s)
        xx = {ph: _dot(mb[ph], tt[ph].astype(BF16)) for ph in heads}
        for ph in heads:
            w = _dot(tt[ph].astype(BF16), xx[ph].astype(BF16))
            tt[ph] = tt[ph] + jnp.where(level, w, 0.0)
        s *= 2

    v_stack = {dp: stack_heads(vv[dp]) for dp in dps}
    mv = {(d, p): _dot(jnp.concatenate([mak[d, p, 0], mak[d, p, 1]], axis=1), v_stack[d, p]) for d, p in dps}
    wu = {}
    for d, p in dps:
        x_rhs = jnp.concatenate([jnp.concatenate(list(a_h[d, p]), axis=0),
                                 stack_heads(mv[d, p].astype(BF16))], axis=1)
        t_cat = jnp.concatenate([tt[d, p, 0].astype(BF16), tt[d, p, 1].astype(BF16)], axis=1)
        wu[d, p] = _dot(t_cat, x_rhs)
    u = {dp: _dot((wu[dp][:, 0:LANES] * em[dp]).astype(BF16), z0b[dp]) + wu[dp][:, LANES:2 * LANES]
         for dp in dps}
    ub = {dp: u[dp].astype(BF16) for dp in dps}
    for d, p in dps:
        y = (_dot((r_t[d, p].astype(F32) * em[d, p]).astype(BF16), z0b[d, p])
             + _dot(jnp.concatenate([arb[d, p, 0], arb[d, p, 1]], axis=1), stack_heads(ub[d, p]))
             + _dot(jnp.concatenate([ark[d, p, 0], ark[d, p, 1]], axis=1), v_stack[d, p]))
        y_refs[d][0, :, sls[p]] = y
    for d, p in dps:
        bhat_t = jnp.transpose(b_t[d, p].astype(F32) * epl[d, p]).astype(BF16)
        khat_t = jnp.transpose(k_t[d, p].astype(F32) * epl[d, p]).astype(BF16)
        upd = _dot(jnp.concatenate([bhat_t, khat_t], axis=1), jnp.concatenate([ub[d, p], vv[d, p]], axis=0))
        pl_col = jnp.sum(jnp.where(eye_c, pl_row[d, p], 0.0), axis=1, keepdims=True)
        z_ref[d, p] = pl_col * z0[d, p] + jnp.where(same_head, upd, 0.0)

    sfin_ref[0] = z_ref[...]


def _rwkv(at, bt, kt, rt, vo, vec, s0, chunk):
    b, _, t, _ = at.shape
    nc = t // chunk
    fwd_tok = pl.BlockSpec((1, 1, chunk, B_W), lambda bi, i: (bi, 0, i, 0))
    rev_tok = pl.BlockSpec((1, 1, chunk, B_W), lambda bi, i: (bi, 1, nc - 1 - i, 0))
    fwd_v = pl.BlockSpec((1, chunk, B_W), lambda bi, i: (bi, i, 0))
    rev_v = pl.BlockSpec((1, chunk, B_W), lambda bi, i: (bi, nc - 1 - i, 0))
    fwd_vec = pl.BlockSpec((1, 1, 1, 8, B_W), lambda bi, i: (bi, 0, i, 0, 0))
    rev_vec = pl.BlockSpec((1, 1, 1, 8, B_W), lambda bi, i: (bi, 1, nc - 1 - i, 0, 0))
    state = pl.BlockSpec((1, 2, N_PAIRS, LANES, LANES), lambda bi, i: (bi, 0, 0, 0, 0))
    kern = functools.partial(_rwkv_kernel, chunk=chunk)
    return pl.pallas_call(
        kern,
        out_shape=(jax.ShapeDtypeStruct((b, t, B_W), F32),
                   jax.ShapeDtypeStruct((b, t, B_W), F32),
                   jax.ShapeDtypeStruct((b, 2, N_PAIRS, LANES, LANES), F32)),
        grid=(b, nc),
        in_specs=[fwd_tok, fwd_tok, fwd_tok, fwd_tok, fwd_v, fwd_vec,
                  rev_tok, rev_tok, rev_tok, rev_tok, rev_v, rev_vec, state],
        out_specs=(fwd_v, rev_v, state),
        scratch_shapes=[pltpu.VMEM((2, N_PAIRS, LANES, LANES), F32)],
        compiler_params=_cparams(("parallel", "arbitrary")),
        name="rwkv",
    )(at, bt, kt, rt, vo, vec, at, bt, kt, rt, vo, vec, s0)


def _post_kernel(y0_ref, y1_ref, bv_ref, gate_ref, u_ref, va_ref, x_ref, mod_ref,
                 lnxg_ref, lnxb_ref, gmg_ref, gmb_ref, wsp_ref, bsp_ref, bd_ref,
                 wout_ref, gpost_ref, gpre_ref, wr_ref, br_ref,
                 x1_ref, h2_ref, comb_ref, *, tm):
    bdm = bd_ref[...]
    inv = 1.0 / B_HEAD

    def group_norm(val, eps):
        mu = _split_dot(val, bdm) * inv
        cen = val - mu
        var = _split_dot(cen * cen, bdm) * inv
        return cen * lax.rsqrt(var + eps)

    y = y0_ref[0] + y1_ref[0]
    yb = group_norm(y, GN_EPS) * lnxg_ref[...] + lnxb_ref[...]
    out_b = (yb + bv_ref[0]) * gate_ref[0]

    uh = jax.nn.gelu(u_ref[0].astype(F32))
    vg = jax.nn.gelu(va_ref[0].astype(F32))
    vn = (group_norm(vg, LN_EPS) * gmg_ref[...] + gmb_ref[...]).astype(BF16)
    lane = lax.broadcasted_iota(jnp.int32, (GM_CHUNK, LANES), 1)
    head0 = lane < B_HEAD
    s_rows = []
    for c in range(tm // GM_CHUNK):
        cols = []
        for p in range(A_W // LANES):
            blk = vn[c * GM_CHUNK:(c + 1) * GM_CHUNK, p * LANES:(p + 1) * LANES]
            zero = jnp.zeros_like(blk)
            stack = jnp.concatenate([jnp.where(head0, blk, zero), jnp.where(head0, zero, blk)], axis=0)
            cols.append(_dot(wsp_ref[p], stack))
        s_rows.append(jnp.concatenate(cols, axis=1) + bsp_ref[...])
    s = jnp.concatenate(s_rows, axis=0)
    out_a = uh * s

    ymix = (_dot(out_a.astype(BF16), wout_ref[0:A_W, :]) + _dot(out_b.astype(BF16), wout_ref[A_W:, :]))
    gate1 = mod_ref[0, 0:1, :]
    shift2 = mod_ref[0, 1:2, :]
    scale2 = mod_ref[0, 2:3, :]
    ms = jnp.mean(ymix * ymix, axis=-1, keepdims=True)
    x1 = x_ref[0] + gate1 * (ymix * lax.rsqrt(ms + NORM_EPS) * gpost_ref[...])
    x1_ref[0] = x1
    ms2 = jnp.mean(x1 * x1, axis=-1, keepdims=True)
    h2 = x1 * lax.rsqrt(ms2 + NORM_EPS) * gpre_ref[...]
    h2 = h2 * (1.0 + scale2) + shift2
    h2_ref[0] = h2.astype(BF16)

    h2_hi = h2.astype(BF16)
    h2_lo = (h2 - h2_hi.astype(F32)).astype(BF16)
    logits = (_dot(h2_hi, wr_ref[0]) + _dot(h2_lo, wr_ref[0]) + _dot(h2_hi, wr_ref[1])) + br_ref[...]
    lni = lax.broadcasted_iota(jnp.int32, logits.shape, 1)
    ln = lni.astype(F32)
    lgrp = (lni // EXPERTS_PER_GROUP).astype(F32)
    neg = jnp.float32(-jnp.inf)
    big = jnp.float32(LANES)
    gmask = (lni >= N_EXPERTS) & (lni < N_EXPERTS + N_GROUPS)
    gl = jnp.where(gmask, logits, neg)
    gmax = jnp.max(gl, axis=-1, keepdims=True)
    gsum = jnp.sum(jnp.where(gmask, jnp.exp(gl - gmax), 0.0), axis=-1, keepdims=True)
    g_w = 1.0 / gsum
    g_i = jnp.min(jnp.where(gl == gmax, ln - N_EXPERTS, big), axis=-1, keepdims=True)
    emask = (lni < N_EXPERTS) & (lgrp == g_i)
    el = jnp.where(emask, logits, neg)
    l1 = jnp.max(el, axis=-1, keepdims=True)
    i1 = jnp.min(jnp.where(el == l1, ln, big), axis=-1, keepdims=True)
    el2 = jnp.where(ln == i1, neg, el)
    l2 = jnp.max(el2, axis=-1, keepdims=True)
    i2 = jnp.min(jnp.where(el2 == l2, ln, big), axis=-1, keepdims=True)
    e21 = jnp.exp(l2 - l1)
    w1 = 1.0 / (1.0 + e21)
    w2 = e21 * w1
    comb_ref[0] = g_w * (jnp.where(ln == i1, w1, 0.0) + jnp.where(ln == i2, w2, 0.0))


def _post(y0, y1, bv, gate, u, va, x, mod3, p, tm):
    b, t, _ = x.shape
    tok = lambda width: pl.BlockSpec((1, tm, width), lambda bi, i: (bi, i, 0))
    full = lambda shape: pl.BlockSpec(shape, lambda bi, i: (0,) * len(shape))
    kern = functools.partial(_post_kernel, tm=tm)
    return pl.pallas_call(
        kern,
        out_shape=(jax.ShapeDtypeStruct((b, t, D_MODEL), F32),
                   jax.ShapeDtypeStruct((b, t, D_MODEL), BF16),
                   jax.ShapeDtypeStruct((b, t, LANES), F32)),
        grid=(b, t // tm),
        in_specs=[tok(B_W), tok(B_W), tok(B_W), tok(B_W), tok(A_W), tok(A_W), tok(D_MODEL),
                  pl.BlockSpec((1, 8, D_MODEL), lambda bi, i: (bi, 0, 0)),
                  full((1, B_W)), full((1, B_W)), full((1, A_W)), full((1, A_W)),
                  full((A_W // LANES, GM_CHUNK, 2 * GM_CHUNK)), full((GM_CHUNK, A_W)), full((B_W, B_W)),
                  full((D_MODEL, D_MODEL)), full((1, D_MODEL)), full((1, D_MODEL)),
                  full((2, D_MODEL, LANES)), full((1, LANES))],
        out_specs=(tok(D_MODEL), tok(D_MODEL), tok(LANES)),
        compiler_params=_cparams(("parallel", "parallel")),
        name="post",
    )(y0, y1, bv, gate, u, va, x, mod3, p["lnx_g"], p["lnx_b"], p["gm_ln_g"], p["gm_ln_b"],
      p["wsp"], p["bsp"], p["bd"], p["w_out"], p["g_post1"], p["g_pre2"], p["w_router"], p["b_router"])


MOE_EB = 4


def _moe_kernel(h_ref, comb_ref, x1_ref, mod_ref, wg_ref, wu_ref, wd_ref, gpost_ref, o_ref, acc_ref):
    g = pl.program_id(2)
    h = h_ref[0]
    comb = comb_ref[0]
    ln = lax.broadcasted_iota(jnp.int32, comb.shape, 1)
    parts = []
    for j in range(MOE_EB):
        gate = _dot(h, wg_ref[j])
        up = _dot(h, wu_ref[j])
        ce = jnp.sum(jnp.where(ln == g * MOE_EB + j, comb, 0.0), axis=-1, keepdims=True)
        parts.append((gate * jax.nn.sigmoid(gate) * up * ce).astype(BF16))
    contrib = _dot(jnp.concatenate(parts, axis=1), wd_ref[0])

    @pl.when(g == 0)
    def _():
        acc_ref[...] = contrib

    @pl.when(g > 0)
    def _():
        acc_ref[...] += contrib

    @pl.when(g == pl.num_programs(2) - 1)
    def _():
        y = acc_ref[...]
        ms = jnp.mean(y * y, axis=-1, keepdims=True)
        o_ref[0] = x1_ref[0] + mod_ref[0, 3:4, :] * (y * lax.rsqrt(ms + NORM_EPS) * gpost_ref[...])


def _moe(h2, comb, x1, mod3, wg, wu, wd, g_post2, tm):
    b, t, _ = x1.shape
    ng = N_EXPERTS // MOE_EB
    wide = MOE_EB * D_EXPERT
    w_in_spec = pl.BlockSpec((MOE_EB, D_MODEL, D_EXPERT), lambda bi, i, e: (e, 0, 0))
    tok = lambda width: pl.BlockSpec((1, tm, width), lambda bi, i, e: (bi, i, 0))
    return pl.pallas_call(
        _moe_kernel,
        out_shape=jax.ShapeDtypeStruct((b, t, D_MODEL), F32),
        grid=(b, t // tm, ng),
        in_specs=[tok(D_MODEL), tok(LANES), tok(D_MODEL),
                  pl.BlockSpec((1, 8, D_MODEL), lambda bi, i, e: (bi, 0, 0)),
                  w_in_spec, w_in_spec,
                  pl.BlockSpec((1, wide, D_MODEL), lambda bi, i, e: (e, 0, 0)),
                  pl.BlockSpec((1, D_MODEL), lambda bi, i, e: (0, 0))],
        out_specs=tok(D_MODEL),
        scratch_shapes=[pltpu.VMEM((tm, D_MODEL), F32)],
        compiler_params=_cparams(("parallel", "parallel", "arbitrary")),
        name="moe",
    )(h2, comb, x1, mod3, wg, wu, wd, g_post2.reshape(1, D_MODEL))


def _tile(t, pref):
    while t % pref:
        pref //= 2
    return pref


def kernel(x, c, ctx, c_ctx, w_mod, b_mod, g_pre1, g_post1, g_pre2, g_post2, w_in, conv_rkv, gm_ln_g, gm_ln_b, w_spatial, b_spatial, decay_w0, decay_up, iclr_a0, iclr_up, k_k, k_a, r_k, outgate_up, lnx_g, lnx_b, w_out, w_router_grp, b_router_grp, w_router_exp, b_router_exp, w_gate, w_up, w_down):
    assert w_mod.shape[0] == 1, "single-layer kernel"
    b, t, _ = x.shape
    tc = ctx.shape[1]
    assert b <= 7 and t % RW_CHUNK == 0 and tc % RW_CHUNK == 0 and t % GRID_W == 0

    head_of = np.arange(B_W) // B_HEAD
    bd = jnp.asarray(head_of[:, None] == head_of[None, :], BF16)
    zpad = jnp.zeros((2, DECAY_LORA, B_W), F32)
    wdec = jnp.stack([jnp.concatenate([decay_up[0, 0], zpad[0]], 0),
                      jnp.concatenate([zpad[0], decay_up[0, 1]], 0)]).astype(BF16)
    wicl = jnp.stack([jnp.concatenate([iclr_up[0, 0], zpad[0]], 0),
                      jnp.concatenate([zpad[0], iclr_up[0, 1]], 0)]).astype(BF16)
    wsp = w_spatial[0].reshape(A_W // LANES, 2, GM_CHUNK, GM_CHUNK)
    wsp = jnp.concatenate([wsp[:, 0], wsp[:, 1]], axis=-1).astype(BF16)
    bsp = jnp.repeat(b_spatial[0].T, A_W // A_GROUPS, axis=1)
    w_router = jnp.concatenate(
        [w_router_exp[0], w_router_grp[0],
         jnp.zeros((D_MODEL, LANES - N_EXPERTS - N_GROUPS), F32)], axis=1)
    w_router_hi = w_router.astype(BF16)
    w_router = jnp.stack([w_router_hi, (w_router - w_router_hi.astype(F32)).astype(BF16)])
    b_router = jnp.concatenate(
        [b_router_exp[0], b_router_grp[0], jnp.zeros((LANES - N_EXPERTS - N_GROUPS,), F32)]).reshape(1, LANES)
    prm = {
        "k_k": k_k[0].reshape(1, B_W), "decay_w0": decay_w0[0], "wdec": wdec,
        "iclr_a0": iclr_a0[0], "wicl": wicl, "k_a": k_a[0], "r_k": r_k[0].reshape(1, B_W),
        "wgate": outgate_up[0].astype(BF16), "bd": bd,
        "lnx_g": lnx_g[0].reshape(1, B_W), "lnx_b": lnx_b[0].reshape(1, B_W),
        "gm_ln_g": gm_ln_g[0].reshape(1, A_W), "gm_ln_b": gm_ln_b[0].reshape(1, A_W),
        "wsp": wsp, "bsp": bsp, "w_out": w_out[0].astype(BF16),
        "g_post1": g_post1[0].reshape(1, D_MODEL), "g_pre2": g_pre2[0].reshape(1, D_MODEL),
        "w_router": w_router, "b_router": b_router,
    }
    w_in_b = w_in[0].astype(BF16)

    cond8 = jnp.concatenate([c, c_ctx[None, :], jnp.zeros((8 - b - 1, D_MODEL), F32)], axis=0)
    mod = _mod(cond8, w_mod[0], b_mod[0])
    mod6 = mod.reshape(8, 6, D_MODEL)
    shift1, scale1 = mod6[:b, 0:1], mod6[:b, 1:2]
    cshift1 = jnp.broadcast_to(mod6[b:b + 1, 0:1], (b, 1, D_MODEL))
    cscale1 = jnp.broadcast_to(mod6[b:b + 1, 1:2], (b, 1, D_MODEL))
    mod3 = jnp.concatenate([mod6[:b, 2:6], jnp.zeros((b, 4, D_MODEL), F32)], axis=1)

    c_rkv, c_lora, c_gd, _, _ = _inproj(ctx, g_pre1[0], cshift1, cscale1, w_in_b, _tile(tc, 256))
    c_kvr = _conv(c_rkv, conv_rkv[0], tc, _tile(tc, 256), 3 * B_W)
    c_at, c_bt, c_kt, c_rt, c_v, c_vec, _, _ = _prep(c_kvr, c_lora, c_gd, prm, _tile(tc, 256), RW_CHUNK)
    s_zero = jnp.zeros((b, 2, N_PAIRS, LANES, LANES), F32)
    _, _, states = _rwkv(c_at, c_bt, c_kt, c_rt, c_v, c_vec, s_zero, RW_CHUNK)

    rkv, lora, gd, u, va = _inproj(x, g_pre1[0], shift1, scale1, w_in_b, _tile(t, 512))
    kvr = _conv(rkv, conv_rkv[0], GRID_W, _tile(t, 512), 3 * B_W)
    at, bt, kt, rt, vo, vec, bv, gate = _prep(kvr, lora, gd, prm, _tile(t, 256), RW_CHUNK)
    y0, y1, _ = _rwkv(at, bt, kt, rt, vo, vec, states, RW_CHUNK)
    x1, h2, comb = _post(y0, y1, bv, gate, u, va, x, mod3, prm, _tile(t, 256))

    wd = w_down[0].astype(BF16).reshape(N_EXPERTS // MOE_EB, MOE_EB * D_EXPERT, D_MODEL)
    return _moe(h2, comb, x1, mod3, w_gate[0].astype(BF16), w_up[0].astype(BF16), wd, g_post2[0],
                _tile(t, 1024))
```

```python
import functools
import math

import numpy as np
import jax
import jax.numpy as jnp
from jax import lax
from jax.experimental import pallas as pl
from jax.experimental.pallas import tpu as pltpu

D_MODEL = 1024
GRID_W = 64
A_W = 512
A_GROUPS = 8
GM_CHUNK = 128
B_W = 512
B_HEADS = 8
B_HEAD = 64
DECAY_LORA = 64
AAA_LORA = 64
GATE_LORA = 128
OFF_WD = 3 * B_W
OFF_GD = OFF_WD + 2 * DECAY_LORA + 2 * AAA_LORA
OFF_U = OFF_GD + GATE_LORA
OFF_VA = OFF_U + A_W
D_IN = OFF_VA + A_W
N_GROUPS = 4
EXPERTS_PER_GROUP = 8
N_EXPERTS = 32
D_EXPERT = 256
NORM_EPS = 1e-6
LN_EPS = 1e-5
GN_EPS = 64e-5

LANES = 128
SUBLANES = 8
RW_CHUNK = 128
RW_NB = 2
N_PAIRS = B_W // LANES
VMEM_LIMIT = 48 * 1024 * 1024

BF16 = jnp.bfloat16
F32 = jnp.float32


def _dot(a, b):
    return jnp.dot(a, b, preferred_element_type=F32)


def _dot_nt(a, b):
    return lax.dot_general(a, b, (((1,), (1,)), ((), ())), preferred_element_type=F32)


def _split_dot(x, w_bf16):
    hi = x.astype(BF16)
    lo = (x - hi.astype(F32)).astype(BF16)
    return _dot(hi, w_bf16) + _dot(lo, w_bf16)


def _cparams(sem):
    return pltpu.CompilerParams(dimension_semantics=sem, vmem_limit_bytes=VMEM_LIMIT)


def _mod_kernel(c_ref, w_ref, b_ref, o_ref):
    c = c_ref[...]
    s = c * jax.nn.sigmoid(c)
    o_ref[...] = jnp.dot(s, w_ref[...], preferred_element_type=F32,
                         precision=lax.Precision.HIGHEST) + b_ref[...]


def _mod(cond8, w_mod, b_mod):
    n = w_mod.shape[1]
    tn = 1024
    return pl.pallas_call(
        _mod_kernel,
        out_shape=jax.ShapeDtypeStruct((8, n), F32),
        grid=(n // tn,),
        in_specs=[pl.BlockSpec((8, D_MODEL), lambda j: (0, 0)),
                  pl.BlockSpec((D_MODEL, tn), lambda j: (0, j)),
                  pl.BlockSpec((1, tn), lambda j: (0, j))],
        out_specs=pl.BlockSpec((8, tn), lambda j: (0, j)),
        compiler_params=_cparams(("arbitrary",)),
        name="mod",
    )(cond8, w_mod, b_mod.reshape(1, n))


def _inproj_kernel(x_ref, g_ref, sh_ref, sc_ref, w_ref, rkv_ref, lora_ref, gd_ref, u_ref, va_ref):
    x = x_ref[0]
    ms = jnp.mean(x * x, axis=-1, keepdims=True)
    h = x * lax.rsqrt(ms + NORM_EPS) * g_ref[...]
    h = h * (1.0 + sc_ref[0]) + sh_ref[0]
    hb = h.astype(BF16)
    rkv_ref[0] = _dot(hb, w_ref[:, 0:OFF_WD]).astype(rkv_ref.dtype)
    lora_ref[0] = _dot(hb, w_ref[:, OFF_WD:OFF_GD])
    gd_ref[0] = _dot(hb, w_ref[:, OFF_GD:OFF_U])
    u_ref[0] = _dot(hb, w_ref[:, OFF_U:OFF_VA]).astype(u_ref.dtype)
    va_ref[0] = _dot(hb, w_ref[:, OFF_VA:D_IN]).astype(va_ref.dtype)


def _inproj(x, g, shift, scale, w_in_bf16, tm):
    b, t, _ = x.shape
    tok = lambda width: pl.BlockSpec((1, tm, width), lambda bi, i: (bi, i, 0))
    vec = pl.BlockSpec((1, 1, D_MODEL), lambda bi, i: (bi, 0, 0))
    return pl.pallas_call(
        _inproj_kernel,
        out_shape=(jax.ShapeDtypeStruct((b, t, OFF_WD), BF16),
                   jax.ShapeDtypeStruct((b, t, OFF_GD - OFF_WD), F32),
                   jax.ShapeDtypeStruct((b, t, GATE_LORA), F32),
                   jax.ShapeDtypeStruct((b, t, A_W), BF16),
                   jax.ShapeDtypeStruct((b, t, A_W), BF16)),
        grid=(b, t // tm),
        in_specs=[tok(D_MODEL),
                  pl.BlockSpec((1, D_MODEL), lambda bi, i: (0, 0)),
                  vec, vec,
                  pl.BlockSpec((D_MODEL, D_IN), lambda bi, i: (0, 0))],
        out_specs=(tok(OFF_WD), tok(OFF_GD - OFF_WD), tok(GATE_LORA), tok(A_W), tok(A_W)),
        compiler_params=_cparams(("parallel", "parallel")),
        name="inproj",
    )(x, g.reshape(1, D_MODEL), shift, scale, w_in_bf16)


CONV_HALO = 128


def _conv_kernel(prev_ref, cur_ref, next_ref, w_ref, o_ref, e_ref, *, width, rows, tm):
    i = pl.program_id(1)
    last = pl.num_programs(1) - 1
    cb = o_ref.shape[-1]
    win = tm + 2 * CONV_HALO
    window = jnp.concatenate([
        jnp.where(i > 0, prev_ref[0].astype(F32), 0.0),
        cur_ref[0].astype(F32),
        jnp.where(i < last, next_ref[0].astype(F32), 0.0)], axis=0)
    spos = i * tm - CONV_HALO + lax.broadcasted_iota(jnp.int32, (win, cb), 0)
    col = spos & (width - 1)
    e_ref[0] = pltpu.roll(jnp.where(col == width - 1, 0.0, window), 1, 0)
    e_ref[1] = window
    e_ref[2] = pltpu.roll(jnp.where(col == 0, 0.0, window), win - 1, 0)
    acc = None
    for dr in (-1, 0, 1):
        if rows == 1 and dr != 0:
            continue
        for dc in (-1, 0, 1):
            xs = e_ref[dc + 1, pl.ds(CONV_HALO + dr * width, tm), :]
            term = xs * w_ref[dr + 1, dc + 1:dc + 2, :]
            acc = term if acc is None else acc + term
    o_ref[0] = acc


def _conv(z, w, width, tm, nch):
    b, t, _ = z.shape
    rows = t // width
    assert width & (width - 1) == 0 and (width + 1 <= CONV_HALO or rows == 1)
    cb = 512
    hb = tm // CONV_HALO
    nhalo = t // CONV_HALO
    kern = functools.partial(_conv_kernel, width=width, rows=rows, tm=tm)
    return pl.pallas_call(
        kern,
        out_shape=jax.ShapeDtypeStruct((b, t, nch), F32),
        grid=(b, t // tm, nch // cb),
        in_specs=[pl.BlockSpec((1, CONV_HALO, cb), lambda bi, i, c: (bi, jnp.maximum(i * hb - 1, 0), c)),
                  pl.BlockSpec((1, tm, cb), lambda bi, i, c: (bi, i, c)),
                  pl.BlockSpec((1, CONV_HALO, cb), lambda bi, i, c: (bi, jnp.minimum((i + 1) * hb, nhalo - 1), c)),
                  pl.BlockSpec((3, 3, cb), lambda bi, i, c: (0, 0, c))],
        out_specs=pl.BlockSpec((1, tm, cb), lambda bi, i, c: (bi, i, c)),
        scratch_shapes=[pltpu.VMEM((3, tm + 2 * CONV_HALO, cb), F32)],
        compiler_params=_cparams(("parallel", "parallel", "parallel")),
        name="conv",
    )(z, z, z, w)


def _prep_kernel(k_ref, v_ref, r_ref, lora_ref, gd_ref, kk_w_ref, w0_ref, wdec_ref, a0_ref, wicl_ref,
                 ka_ref, rk_ref, wgate_ref, bd_ref, dmat_ref, smat_ref,
                 at_ref, bt_ref, kt_ref, rt_ref, vo_ref, vec_ref, bv_ref, gate_ref):
    k = k_ref[0]
    v = v_ref[0]
    r = r_ref[0]
    lora = lora_ref[0]
    bd = bd_ref[...]
    kk = k * kk_w_ref[...]
    ss = _split_dot(kk * kk, bd)
    kk = kk * lax.rsqrt(jnp.maximum(ss, 1e-24))
    tl = jnp.tanh(lora[:, 0:2 * DECAY_LORA]).astype(BF16)
    la = lora[:, 2 * DECAY_LORA:].astype(BF16)
    kmod_sum = jnp.zeros_like(k)
    for d in range(2):
        logw = -math.exp(-0.5) * jax.nn.sigmoid(w0_ref[d:d + 1, :] + _dot(tl, wdec_ref[d]))
        a = jax.nn.sigmoid(a0_ref[d:d + 1, :] + _dot(la, wicl_ref[d]))
        kmod = k * (1.0 + (a - 1.0) * ka_ref[d:d + 1, :])
        kmod_sum = kmod_sum + kmod
        bvec = -(kk * a)
        rel = _split_dot_lhs(dmat_ref[d], logw)
        e_pos = jnp.exp(rel)
        e_neg = jnp.exp(-rel)
        at_ref[0, d] = (kk * jnp.exp(rel - logw)).astype(BF16)
        bt_ref[0, d] = (bvec * e_neg).astype(BF16)
        kt_ref[0, d] = (kmod * e_neg).astype(BF16)
        rt_ref[0, d] = (r * e_pos).astype(BF16)
        vec_ref[0, d, 0] = jnp.exp(_split_dot_lhs(smat_ref[d], logw))
    vo_ref[0] = v.astype(BF16)
    bonus = _split_dot(r * kmod_sum * rk_ref[...], bd)
    bv_ref[0] = bonus * v
    gate_ref[0] = _dot(jax.nn.sigmoid(gd_ref[0]).astype(BF16), wgate_ref[...])


def _split_dot_lhs(m_bf16, x):
    hi = x.astype(BF16)
    lo = (x - hi.astype(F32)).astype(BF16)
    return _dot(m_bf16, hi) + _dot(m_bf16, lo)


def _chunk_matrices(tm, chunk):
    t = np.arange(tm)
    same = (t[:, None] // chunk) == (t[None, :] // chunk)
    start = (t // chunk) * chunk
    half = chunk // 2
    d_fwd = same * ((t[None, :] <= t[:, None]).astype(np.float32)
                    - (t[None, :] < (start + half)[:, None]).astype(np.float32))
    d_rev = same * ((t[None, :] >= t[:, None]).astype(np.float32)
                    - (t[None, :] >= (start + half)[:, None]).astype(np.float32))
    nct = tm // chunk
    s_fwd = np.zeros((nct * 8, tm), np.float32)
    s_rev = np.zeros((nct * 8, tm), np.float32)
    for j in range(nct):
        in_chunk = (t // chunk) == j
        first = in_chunk & (t < j * chunk + half)
        second = in_chunk & (t >= j * chunk + half)
        s_fwd[8 * j + 0] = first
        s_fwd[8 * j + 1] = in_chunk
        s_fwd[8 * j + 2] = second
        s_rev[8 * j + 0] = second
        s_rev[8 * j + 1] = in_chunk
        s_rev[8 * j + 2] = first
    dmat = np.stack([d_fwd, d_rev]).astype(np.float32)
    smat = np.stack([s_fwd, s_rev])
    return jnp.asarray(dmat, BF16), jnp.asarray(smat, BF16)


def _prep(kvr, lora, gd, p, tm, chunk):
    b, t, _ = kvr.shape
    nct = tm // chunk
    dmat, smat = _chunk_matrices(tm, chunk)
    tokc = lambda c: pl.BlockSpec((1, tm, B_W), lambda bi, i: (bi, i, c))
    tok = lambda width: pl.BlockSpec((1, tm, width), lambda bi, i: (bi, i, 0))
    full = lambda shape: pl.BlockSpec(shape, lambda bi, i: (0,) * len(shape))
    dir_tok = pl.BlockSpec((1, 2, tm, B_W), lambda bi, i: (bi, 0, i, 0))
    dir_shape = jax.ShapeDtypeStruct((b, 2, t, B_W), BF16)
    outs = pl.pallas_call(
        _prep_kernel,
        out_shape=(dir_shape, dir_shape, dir_shape, dir_shape,
                   jax.ShapeDtypeStruct((b, t, B_W), BF16),
                   jax.ShapeDtypeStruct((b, 2, t // tm, nct * 8, B_W), F32),
                   jax.ShapeDtypeStruct((b, t, B_W), F32),
                   jax.ShapeDtypeStruct((b, t, B_W), F32)),
        grid=(b, t // tm),
        in_specs=[tokc(0), tokc(1), tokc(2), tok(OFF_GD - OFF_WD), tok(GATE_LORA),
                  full((1, B_W)), full((2, B_W)), full((2, 2 * DECAY_LORA, B_W)),
                  full((2, B_W)), full((2, 2 * AAA_LORA, B_W)), full((2, B_W)), full((1, B_W)),
                  full((GATE_LORA, B_W)), full((B_W, B_W)),
                  full((2, tm, tm)), full((2, nct * 8, tm))],
        out_specs=(dir_tok, dir_tok, dir_tok, dir_tok, tok(B_W),
                   pl.BlockSpec((1, 2, 1, nct * 8, B_W), lambda bi, i: (bi, 0, i, 0, 0)),
                   tok(B_W), tok(B_W)),
        compiler_params=_cparams(("parallel", "parallel")),
        name="prep",
    )(kvr, kvr, kvr, lora, gd, p["k_k"], p["decay_w0"], p["wdec"], p["iclr_a0"], p["wicl"],
      p["k_a"], p["r_k"], p["wgate"], p["bd"], dmat, smat)
    at, bt, kt, rt, vo, vec, bv, gate = outs
    vec = vec.reshape(b, 2, t // chunk, 8, B_W)
    return at, bt, kt, rt, vo, vec, bv, gate


def _rwkv_kernel(at0, bt0, kt0, rt0, v0, vec0, at1, bt1, kt1, rt1, v1, vec1, s0_ref,
                 y0_ref, y1_ref, sfin_ref, z_ref, *, chunk, nb):
    i = pl.program_id(1)
    L = chunk
    assert L & (L - 1) == 0
    a_refs, b_refs, k_refs, r_refs = (at0, at1), (bt0, bt1), (kt0, kt1), (rt0, rt1)
    v_refs, vec_refs, y_refs = (v0, v1), (vec0, vec1), (y0_ref, y1_ref)

    @pl.when(i == 0)
    def _():
        z_ref[...] = s0_ref[...]

    rowi = lax.broadcasted_iota(jnp.int32, (L, L), 0)
    coli = lax.broadcasted_iota(jnp.int32, (L, L), 1)
    strict = (rowi > coli, rowi < coli)
    incl = (rowi >= coli, rowi <= coli)
    eye_f = (rowi == coli).astype(F32)
    blk_xor = rowi ^ coli
    lane = lax.broadcasted_iota(jnp.int32, (L, LANES), 1)
    head0 = lane < B_HEAD
    sub = lax.broadcasted_iota(jnp.int32, (LANES, LANES), 0)
    lan = lax.broadcasted_iota(jnp.int32, (LANES, LANES), 1)
    same_head = (sub < B_HEAD) == (lan < B_HEAD)
    eye_c = sub == lan

    def split_heads(x):
        zero = jnp.zeros_like(x)
        return jnp.where(head0, x, zero), jnp.where(head0, zero, x)

    def stack_heads(x):
        x0, x1 = split_heads(x)
        return jnp.concatenate([x0, x1], axis=0)

    sls = [slice(p * LANES, (p + 1) * LANES) for p in range(N_PAIRS)]
    dps = [(n, d, p) for n in range(nb) for d in (0, 1) for p in range(N_PAIRS)]
    a_t = {(n, d, p): a_refs[d][n, 0, :, sls[p]] for n, d, p in dps}
    b_t = {(n, d, p): b_refs[d][n, 0, :, sls[p]] for n, d, p in dps}
    k_t = {(n, d, p): k_refs[d][n, 0, :, sls[p]] for n, d, p in dps}
    r_t = {(n, d, p): r_refs[d][n, 0, :, sls[p]] for n, d, p in dps}
    vv = {(n, d, p): v_refs[d][n, :, sls[p]] for n, d, p in dps}
    em = {(n, d, p): vec_refs[d][n, 0, 0, 0:1, sls[p]] for n, d, p in dps}
    pl_row = {(n, d, p): vec_refs[d][n, 0, 0, 1:2, sls[p]] for n, d, p in dps}
    epl = {(n, d, p): vec_refs[d][n, 0, 0, 2:3, sls[p]] for n, d, p in dps}
    z0 = {dp: z_ref[dp] for dp in dps}
    z0b = {dp: z0[dp].astype(BF16) for dp in dps}
    a_h = {dp: split_heads(a_t[dp]) for dp in dps}
    r_h = {dp: split_heads(r_t[dp]) for dp in dps}

    heads = [dp + (h,) for dp in dps for h in range(2)]
    tt, mf, mak, arb, ark = {}, {}, {}, {}, {}
    for ph in heads:
        dp, d, h = ph[:3], ph[1], ph[3]
        bk = jnp.concatenate([b_t[dp], k_t[dp]], axis=0)
        res = _dot_nt(jnp.concatenate([a_h[dp][h], r_h[dp][h]], axis=0), bk)
        mab = jnp.where(strict[d], res[0:L, 0:L], 0.0)
        mak[ph] = jnp.where(strict[d], res[0:L, L:2 * L], 0.0).astype(BF16)
        arb[ph] = jnp.where(incl[d], res[L:2 * L, 0:L], 0.0).astype(BF16)
        ark[ph] = jnp.where(incl[d], res[L:2 * L, L:2 * L], 0.0).astype(BF16)
        tt[ph] = eye_f + jnp.where(blk_xor == 1, mab, 0.0)
        mf[ph] = mab
    def take(x, s, d):
        if s % SUBLANES:
            return x
        first = 1 - d
        return jnp.concatenate([x[(2 * j + first) * s:(2 * j + first + 1) * s] for j in range(L // (2 * s))],
                               axis=0)

    def put(x, s, d):
        if s % SUBLANES:
            return x
        zero = jnp.zeros((s, x.shape[1]), x.dtype)
        pieces = []
        for j in range(L // (2 * s)):
            blk = x[j * s:(j + 1) * s]
            pieces += [zero, blk] if d == 0 else [blk, zero]
        return jnp.concatenate(pieces, axis=0)

    s = 2
    while s < L:
        level = (blk_xor >= s) & (blk_xor < 2 * s)
        xx = {ph: _dot(take(mf[ph], s, ph[1]).astype(BF16), tt[ph].astype(BF16)) for ph in heads}
        for ph in heads:
            w = _dot(take(tt[ph], s, ph[1]).astype(BF16), put(xx[ph], s, ph[1]).astype(BF16))
            tt[ph] = tt[ph] + jnp.where(level, put(w, s, ph[1]), 0.0)
        s *= 2

    v_stack = {dp: stack_heads(vv[dp]) for dp in dps}
    mv = {dp: _dot(jnp.concatenate([mak[dp + (0,)], mak[dp + (1,)]], axis=1), v_stack[dp]) for dp in dps}
    wu = {}
    for dp in dps:
        x_rhs = jnp.concatenate([jnp.concatenate(list(a_h[dp]), axis=0),
                                 stack_heads(mv[dp].astype(BF16))], axis=1)
        t_cat = jnp.concatenate([tt[dp + (0,)].astype(BF16), tt[dp + (1,)].astype(BF16)], axis=1)
        wu[dp] = _dot(t_cat, x_rhs)
    u = {dp: _dot((wu[dp][:, 0:LANES] * em[dp]).astype(BF16), z0b[dp]) + wu[dp][:, LANES:2 * LANES]
         for dp in dps}
    ub = {dp: u[dp].astype(BF16) for dp in dps}
    for dp in dps:
        n, d, p = dp
        y = (_dot((r_t[dp].astype(F32) * em[dp]).astype(BF16), z0b[dp])
             + _dot(jnp.concatenate([arb[dp + (0,)], arb[dp + (1,)]], axis=1), stack_heads(ub[dp]))
             + _dot(jnp.concatenate([ark[dp + (0,)], ark[dp + (1,)]], axis=1), v_stack[dp]))
        y_refs[d][n, :, sls[p]] = y
    for dp in dps:
        bhat_t = jnp.transpose(b_t[dp].astype(F32) * epl[dp]).astype(BF16)
        khat_t = jnp.transpose(k_t[dp].astype(F32) * epl[dp]).astype(BF16)
        upd = _dot(jnp.concatenate([bhat_t, khat_t], axis=1), jnp.concatenate([ub[dp], vv[dp]], axis=0))
        pl_col = jnp.sum(jnp.where(eye_c, pl_row[dp], 0.0), axis=1, keepdims=True)
        z_ref[dp] = pl_col * z0[dp] + jnp.where(same_head, upd, 0.0)

    sfin_ref[...] = z_ref[...]


def _rwkv(at, bt, kt, rt, vo, vec, s0, chunk):
    b, _, t, _ = at.shape
    nc = t // chunk
    nb = RW_NB if b % RW_NB == 0 else 1
    fwd_tok = pl.BlockSpec((nb, 1, chunk, B_W), lambda bi, i: (bi, 0, i, 0))
    rev_tok = pl.BlockSpec((nb, 1, chunk, B_W), lambda bi, i: (bi, 1, nc - 1 - i, 0))
    fwd_v = pl.BlockSpec((nb, chunk, B_W), lambda bi, i: (bi, i, 0))
    rev_v = pl.BlockSpec((nb, chunk, B_W), lambda bi, i: (bi, nc - 1 - i, 0))
    fwd_vec = pl.BlockSpec((nb, 1, 1, 8, B_W), lambda bi, i: (bi, 0, i, 0, 0))
    rev_vec = pl.BlockSpec((nb, 1, 1, 8, B_W), lambda bi, i: (bi, 1, nc - 1 - i, 0, 0))
    state = pl.BlockSpec((nb, 2, N_PAIRS, LANES, LANES), lambda bi, i: (bi, 0, 0, 0, 0))
    kern = functools.partial(_rwkv_kernel, chunk=chunk, nb=nb)
    return pl.pallas_call(
        kern,
        out_shape=(jax.ShapeDtypeStruct((b, t, B_W), F32),
                   jax.ShapeDtypeStruct((b, t, B_W), F32),
                   jax.ShapeDtypeStruct((b, 2, N_PAIRS, LANES, LANES), F32)),
        grid=(b // nb, nc),
        in_specs=[fwd_tok, fwd_tok, fwd_tok, fwd_tok, fwd_v, fwd_vec,
                  rev_tok, rev_tok, rev_tok, rev_tok, rev_v, rev_vec, state],
        out_specs=(fwd_v, rev_v, state),
        scratch_shapes=[pltpu.VMEM((nb, 2, N_PAIRS, LANES, LANES), F32)],
        compiler_params=_cparams(("parallel", "arbitrary")),
        name="rwkv",
    )(at, bt, kt, rt, vo, vec, at, bt, kt, rt, vo, vec, s0)


def _post_kernel(y0_ref, y1_ref, bv_ref, gate_ref, u_ref, va_ref, x_ref, mod_ref,
                 lnxg_ref, lnxb_ref, gmg_ref, gmb_ref, wsp_ref, bsp_ref, bd_ref,
                 wout_ref, gpost_ref, gpre_ref, wr_ref, br_ref,
                 x1_ref, h2_ref, comb_ref, *, tm):
    bdm = bd_ref[...]
    inv = 1.0 / B_HEAD

    def group_norm(val, eps):
        mu = _split_dot(val, bdm) * inv
        cen = val - mu
        var = _dot((cen * cen).astype(BF16), bdm) * inv
        return cen * lax.rsqrt(var + eps)

    y = y0_ref[0] + y1_ref[0]
    yb = group_norm(y, GN_EPS) * lnxg_ref[...] + lnxb_ref[...]
    out_b = (yb + bv_ref[0]) * gate_ref[0]

    uh = jax.nn.gelu(u_ref[0].astype(F32))
    vg = jax.nn.gelu(va_ref[0].astype(F32))
    vn = (group_norm(vg, LN_EPS) * gmg_ref[...] + gmb_ref[...]).astype(BF16)
    lane = lax.broadcasted_iota(jnp.int32, (GM_CHUNK, LANES), 1)
    head0 = lane < B_HEAD
    s_rows = []
    for c in range(tm // GM_CHUNK):
        cols = []
        for p in range(A_W // LANES):
            blk = vn[c * GM_CHUNK:(c + 1) * GM_CHUNK, p * LANES:(p + 1) * LANES]
            zero = jnp.zeros_like(blk)
            stack = jnp.concatenate([jnp.where(head0, blk, zero), jnp.where(head0, zero, blk)], axis=0)
            cols.append(_dot(wsp_ref[p], stack))
        s_rows.append(jnp.concatenate(cols, axis=1) + bsp_ref[...])
    s = jnp.concatenate(s_rows, axis=0)
    out_a = uh * s

    ymix = (_dot(out_a.astype(BF16), wout_ref[0:A_W, :]) + _dot(out_b.astype(BF16), wout_ref[A_W:, :]))
    gate1 = mod_ref[0, 0:1, :]
    shift2 = mod_ref[0, 1:2, :]
    scale2 = mod_ref[0, 2:3, :]
    ms = jnp.mean(ymix * ymix, axis=-1, keepdims=True)
    x1 = x_ref[0] + gate1 * (ymix * lax.rsqrt(ms + NORM_EPS) * gpost_ref[...])
    x1_ref[0] = x1
    ms2 = jnp.mean(x1 * x1, axis=-1, keepdims=True)
    h2 = x1 * lax.rsqrt(ms2 + NORM_EPS) * gpre_ref[...]
    h2 = h2 * (1.0 + scale2) + shift2
    h2_ref[0] = h2.astype(BF16)

    h2_hi = h2.astype(BF16)
    h2_lo = (h2 - h2_hi.astype(F32)).astype(BF16)
    logits = (_dot(h2_hi, wr_ref[0]) + _dot(h2_lo, wr_ref[0]) + _dot(h2_hi, wr_ref[1])) + br_ref[...]
    lni = lax.broadcasted_iota(jnp.int32, logits.shape, 1)
    ln = lni.astype(F32)
    lgrp = (lni // EXPERTS_PER_GROUP).astype(F32)
    neg = jnp.float32(-jnp.inf)
    big = jnp.float32(LANES)
    gmask = (lni >= N_EXPERTS) & (lni < N_EXPERTS + N_GROUPS)
    gl = jnp.where(gmask, logits, neg)
    gmax = jnp.max(gl, axis=-1, keepdims=True)
    gsum = jnp.sum(jnp.where(gmask, jnp.exp(gl - gmax), 0.0), axis=-1, keepdims=True)
    g_w = 1.0 / gsum
    g_i = jnp.min(jnp.where(gl == gmax, ln - N_EXPERTS, big), axis=-1, keepdims=True)
    emask = (lni < N_EXPERTS) & (lgrp == g_i)
    el = jnp.where(emask, logits, neg)
    l1 = jnp.max(el, axis=-1, keepdims=True)
    i1 = jnp.min(jnp.where(el == l1, ln, big), axis=-1, keepdims=True)
    el2 = jnp.where(ln == i1, neg, el)
    l2 = jnp.max(el2, axis=-1, keepdims=True)
    i2 = jnp.min(jnp.where(el2 == l2, ln, big), axis=-1, keepdims=True)
    e21 = jnp.exp(l2 - l1)
    w1 = 1.0 / (1.0 + e21)
    w2 = e21 * w1
    comb_ref[0] = g_w * (jnp.where(ln == i1, w1, 0.0) + jnp.where(ln == i2, w2, 0.0))


def _post(y0, y1, bv, gate, u, va, x, mod3, p, tm):
    b, t, _ = x.shape
    tok = lambda width: pl.BlockSpec((1, tm, width), lambda bi, i: (bi, i, 0))
    full = lambda shape: pl.BlockSpec(shape, lambda bi, i: (0,) * len(shape))
    kern = functools.partial(_post_kernel, tm=tm)
    return pl.pallas_call(
        kern,
        out_shape=(jax.ShapeDtypeStruct((b, t, D_MODEL), F32),
                   jax.ShapeDtypeStruct((b, t, D_MODEL), BF16),
                   jax.ShapeDtypeStruct((b, t, LANES), F32)),
        grid=(b, t // tm),
        in_specs=[tok(B_W), tok(B_W), tok(B_W), tok(B_W), tok(A_W), tok(A_W), tok(D_MODEL),
                  pl.BlockSpec((1, 8, D_MODEL), lambda bi, i: (bi, 0, 0)),
                  full((1, B_W)), full((1, B_W)), full((1, A_W)), full((1, A_W)),
                  full((A_W // LANES, GM_CHUNK, 2 * GM_CHUNK)), full((GM_CHUNK, A_W)), full((B_W, B_W)),
                  full((D_MODEL, D_MODEL)), full((1, D_MODEL)), full((1, D_MODEL)),
                  full((2, D_MODEL, LANES)), full((1, LANES))],
        out_specs=(tok(D_MODEL), tok(D_MODEL), tok(LANES)),
        compiler_params=_cparams(("parallel", "parallel")),
        name="post",
    )(y0, y1, bv, gate, u, va, x, mod3, p["lnx_g"], p["lnx_b"], p["gm_ln_g"], p["gm_ln_b"],
      p["wsp"], p["bsp"], p["bd"], p["w_out"], p["g_post1"], p["g_pre2"], p["w_router"], p["b_router"])


MOE_EB = 4


def _moe_kernel(h_ref, comb_ref, x1_ref, mod_ref, wg_ref, wu_ref, wd_ref, gpost_ref, o_ref, acc_ref):
    g = pl.program_id(2)
    h = h_ref[0]
    comb = comb_ref[0]
    ln = lax.broadcasted_iota(jnp.int32, comb.shape, 1)
    parts = []
    for j in range(MOE_EB):
        gate = _dot(h, wg_ref[j])
        up = _dot(h, wu_ref[j])
        ce = jnp.sum(jnp.where(ln == g * MOE_EB + j, comb, 0.0), axis=-1, keepdims=True)
        parts.append((gate * jax.nn.sigmoid(gate) * up * ce).astype(BF16))
    contrib = _dot(jnp.concatenate(parts, axis=1), wd_ref[0])

    @pl.when(g == 0)
    def _():
        acc_ref[...] = contrib

    @pl.when(g > 0)
    def _():
        acc_ref[...] += contrib

    @pl.when(g == pl.num_programs(2) - 1)
    def _():
        y = acc_ref[...]
        ms = jnp.mean(y * y, axis=-1, keepdims=True)
        o_ref[0] = x1_ref[0] + mod_ref[0, 3:4, :] * (y * lax.rsqrt(ms + NORM_EPS) * gpost_ref[...])


def _moe(h2, comb, x1, mod3, wg, wu, wd, g_post2, tm):
    b, t, _ = x1.shape
    ng = N_EXPERTS // MOE_EB
    wide = MOE_EB * D_EXPERT
    w_in_spec = pl.BlockSpec((MOE_EB, D_MODEL, D_EXPERT), lambda bi, i, e: (e, 0, 0))
    tok = lambda width: pl.BlockSpec((1, tm, width), lambda bi, i, e: (bi, i, 0))
    return pl.pallas_call(
        _moe_kernel,
        out_shape=jax.ShapeDtypeStruct((b, t, D_MODEL), F32),
        grid=(b, t // tm, ng),
        in_specs=[tok(D_MODEL), tok(LANES), tok(D_MODEL),
                  pl.BlockSpec((1, 8, D_MODEL), lambda bi, i, e: (bi, 0, 0)),
                  w_in_spec, w_in_spec,
                  pl.BlockSpec((1, wide, D_MODEL), lambda bi, i, e: (e, 0, 0)),
                  pl.BlockSpec((1, D_MODEL), lambda bi, i, e: (0, 0))],
        out_specs=tok(D_MODEL),
        scratch_shapes=[pltpu.VMEM((tm, D_MODEL), F32)],
        compiler_params=_cparams(("parallel", "parallel", "arbitrary")),
        name="moe",
    )(h2, comb, x1, mod3, wg, wu, wd, g_post2.reshape(1, D_MODEL))


def _tile(t, pref):
    while t % pref:
        pref //= 2
    return pref


def kernel(x, c, ctx, c_ctx, w_mod, b_mod, g_pre1, g_post1, g_pre2, g_post2, w_in, conv_rkv, gm_ln_g, gm_ln_b, w_spatial, b_spatial, decay_w0, decay_up, iclr_a0, iclr_up, k_k, k_a, r_k, outgate_up, lnx_g, lnx_b, w_out, w_router_grp, b_router_grp, w_router_exp, b_router_exp, w_gate, w_up, w_down):
    assert w_mod.shape[0] == 1, "single-layer kernel"
    b, t, _ = x.shape
    tc = ctx.shape[1]
    assert b <= 7 and t % RW_CHUNK == 0 and tc % RW_CHUNK == 0 and t % GRID_W == 0

    head_of = np.arange(B_W) // B_HEAD
    bd = jnp.asarray(head_of[:, None] == head_of[None, :], BF16)
    zpad = jnp.zeros((2, DECAY_LORA, B_W), F32)
    wdec = jnp.stack([jnp.concatenate([decay_up[0, 0], zpad[0]], 0),
                      jnp.concatenate([zpad[0], decay_up[0, 1]], 0)]).astype(BF16)
    wicl = jnp.stack([jnp.concatenate([iclr_up[0, 0], zpad[0]], 0),
                      jnp.concatenate([zpad[0], iclr_up[0, 1]], 0)]).astype(BF16)
    wsp = w_spatial[0].reshape(A_W // LANES, 2, GM_CHUNK, GM_CHUNK)
    wsp = jnp.concatenate([wsp[:, 0], wsp[:, 1]], axis=-1).astype(BF16)
    bsp = jnp.repeat(b_spatial[0].T, A_W // A_GROUPS, axis=1)
    w_router = jnp.concatenate(
        [w_router_exp[0], w_router_grp[0],
         jnp.zeros((D_MODEL, LANES - N_EXPERTS - N_GROUPS), F32)], axis=1)
    w_router_hi = w_router.astype(BF16)
    w_router = jnp.stack([w_router_hi, (w_router - w_router_hi.astype(F32)).astype(BF16)])
    b_router = jnp.concatenate(
        [b_router_exp[0], b_router_grp[0], jnp.zeros((LANES - N_EXPERTS - N_GROUPS,), F32)]).reshape(1, LANES)
    prm = {
        "k_k": k_k[0].reshape(1, B_W), "decay_w0": decay_w0[0], "wdec": wdec,
        "iclr_a0": iclr_a0[0], "wicl": wicl, "k_a": k_a[0], "r_k": r_k[0].reshape(1, B_W),
        "wgate": outgate_up[0].astype(BF16), "bd": bd,
        "lnx_g": lnx_g[0].reshape(1, B_W), "lnx_b": lnx_b[0].reshape(1, B_W),
        "gm_ln_g": gm_ln_g[0].reshape(1, A_W), "gm_ln_b": gm_ln_b[0].reshape(1, A_W),
        "wsp": wsp, "bsp": bsp, "w_out": w_out[0].astype(BF16),
        "g_post1": g_post1[0].reshape(1, D_MODEL), "g_pre2": g_pre2[0].reshape(1, D_MODEL),
        "w_router": w_router, "b_router": b_router,
    }
    w_in_b = w_in[0].astype(BF16)

    cond8 = jnp.concatenate([c, c_ctx[None, :], jnp.zeros((8 - b - 1, D_MODEL), F32)], axis=0)
    mod = _mod(cond8, w_mod[0], b_mod[0])
    mod6 = mod.reshape(8, 6, D_MODEL)
    shift1, scale1 = mod6[:b, 0:1], mod6[:b, 1:2]
    cshift1 = jnp.broadcast_to(mod6[b:b + 1, 0:1], (b, 1, D_MODEL))
    cscale1 = jnp.broadcast_to(mod6[b:b + 1, 1:2], (b, 1, D_MODEL))
    mod3 = jnp.concatenate([mod6[:b, 2:6], jnp.zeros((b, 4, D_MODEL), F32)], axis=1)

    c_rkv, c_lora, c_gd, _, _ = _inproj(ctx, g_pre1[0], cshift1, cscale1, w_in_b, _tile(tc, 256))
    c_kvr = _conv(c_rkv, conv_rkv[0], tc, _tile(tc, 256), 3 * B_W)
    c_at, c_bt, c_kt, c_rt, c_v, c_vec, _, _ = _prep(c_kvr, c_lora, c_gd, prm, _tile(tc, 256), RW_CHUNK)
    s_zero = jnp.zeros((b, 2, N_PAIRS, LANES, LANES), F32)
    _, _, states = _rwkv(c_at, c_bt, c_kt, c_rt, c_v, c_vec, s_zero, RW_CHUNK)

    rkv, lora, gd, u, va = _inproj(x, g_pre1[0], shift1, scale1, w_in_b, _tile(t, 512))
    kvr = _conv(rkv, conv_rkv[0], GRID_W, _tile(t, 512), 3 * B_W)
    at, bt, kt, rt, vo, vec, bv, gate = _prep(kvr, lora, gd, prm, _tile(t, 256), RW_CHUNK)
    y0, y1, _ = _rwkv(at, bt, kt, rt, vo, vec, states, RW_CHUNK)
    x1, h2, comb = _post(y0, y1, bv, gate, u, va, x, mod3, prm, _tile(t, 256))

    wd = w_down[0].astype(BF16).reshape(N_EXPERTS // MOE_EB, MOE_EB * D_EXPERT, D_MODEL)
    return _moe(h2, comb, x1, mod3, w_gate[0].astype(BF16), w_up[0].astype(BF16), wd, g_post2[0],
                _tile(t, 1024))
```

```python
import functools
import math

import numpy as np
import jax
import jax.numpy as jnp
from jax import lax
from jax.experimental import pallas as pl
from jax.experimental.pallas import tpu as pltpu

D_MODEL = 1024
GRID_W = 64
A_W = 512
A_GROUPS = 8
GM_CHUNK = 128
B_W = 512
B_HEADS = 8
B_HEAD = 64
DECAY_LORA = 64
AAA_LORA = 64
GATE_LORA = 128
OFF_WD = 3 * B_W
OFF_GD = OFF_WD + 2 * DECAY_LORA + 2 * AAA_LORA
OFF_U = OFF_GD + GATE_LORA
OFF_VA = OFF_U + A_W
D_IN = OFF_VA + A_W
N_GROUPS = 4
EXPERTS_PER_GROUP = 8
N_EXPERTS = 32
D_EXPERT = 256
NORM_EPS = 1e-6
LN_EPS = 1e-5
GN_EPS = 64e-5

LANES = 128
SUBLANES = 8
RW_CHUNK = 128
RW_NB = 2
N_PAIRS = B_W // LANES
ROW_CHUNKS = D_MODEL // LANES
MOE_TM = 512
MOE_TR = 256
MEMB = 64
VMEM_LIMIT = 48 * 1024 * 1024

BF16 = jnp.bfloat16
F32 = jnp.float32


def _dot(a, b):
    return jnp.dot(a, b, preferred_element_type=F32)


def _dot_nt(a, b):
    return lax.dot_general(a, b, (((1,), (1,)), ((), ())), preferred_element_type=F32)


def _split_dot(x, w_bf16):
    hi = x.astype(BF16)
    lo = (x - hi.astype(F32)).astype(BF16)
    return _dot(hi, w_bf16) + _dot(lo, w_bf16)


def _cparams(sem):
    return pltpu.CompilerParams(dimension_semantics=sem, vmem_limit_bytes=VMEM_LIMIT)


def _mod_kernel(c_ref, w_ref, b_ref, o_ref):
    c = c_ref[...]
    s = c * jax.nn.sigmoid(c)
    o_ref[...] = jnp.dot(s, w_ref[...], preferred_element_type=F32,
                         precision=lax.Precision.HIGHEST) + b_ref[...]


def _mod(cond8, w_mod, b_mod):
    n = w_mod.shape[1]
    tn = 1024
    return pl.pallas_call(
        _mod_kernel,
        out_shape=jax.ShapeDtypeStruct((8, n), F32),
        grid=(n // tn,),
        in_specs=[pl.BlockSpec((8, D_MODEL), lambda j: (0, 0)),
                  pl.BlockSpec((D_MODEL, tn), lambda j: (0, j)),
                  pl.BlockSpec((1, tn), lambda j: (0, j))],
        out_specs=pl.BlockSpec((8, tn), lambda j: (0, j)),
        compiler_params=_cparams(("arbitrary",)),
        name="mod",
    )(cond8, w_mod, b_mod.reshape(1, n))


def _inproj_kernel(x_ref, g_ref, sh_ref, sc_ref, w_ref, rkv_ref, lora_ref, gd_ref, u_ref, va_ref):
    x = x_ref[0]
    ms = jnp.mean(x * x, axis=-1, keepdims=True)
    h = x * lax.rsqrt(ms + NORM_EPS) * g_ref[...]
    h = h * (1.0 + sc_ref[0]) + sh_ref[0]
    hb = h.astype(BF16)
    rkv_ref[0] = _dot(hb, w_ref[:, 0:OFF_WD]).astype(rkv_ref.dtype)
    lora_ref[0] = _dot(hb, w_ref[:, OFF_WD:OFF_GD])
    gd_ref[0] = _dot(hb, w_ref[:, OFF_GD:OFF_U])
    u_ref[0] = _dot(hb, w_ref[:, OFF_U:OFF_VA]).astype(u_ref.dtype)
    va_ref[0] = _dot(hb, w_ref[:, OFF_VA:D_IN]).astype(va_ref.dtype)


def _inproj(x, g, shift, scale, w_in_bf16, tm):
    b, t, _ = x.shape
    tok = lambda width: pl.BlockSpec((1, tm, width), lambda bi, i: (bi, i, 0))
    vec = pl.BlockSpec((1, 1, D_MODEL), lambda bi, i: (bi, 0, 0))
    return pl.pallas_call(
        _inproj_kernel,
        out_shape=(jax.ShapeDtypeStruct((b, t, OFF_WD), BF16),
                   jax.ShapeDtypeStruct((b, t, OFF_GD - OFF_WD), F32),
                   jax.ShapeDtypeStruct((b, t, GATE_LORA), F32),
                   jax.ShapeDtypeStruct((b, t, A_W), BF16),
                   jax.ShapeDtypeStruct((b, t, A_W), BF16)),
        grid=(b, t // tm),
        in_specs=[tok(D_MODEL),
                  pl.BlockSpec((1, D_MODEL), lambda bi, i: (0, 0)),
                  vec, vec,
                  pl.BlockSpec((D_MODEL, D_IN), lambda bi, i: (0, 0))],
        out_specs=(tok(OFF_WD), tok(OFF_GD - OFF_WD), tok(GATE_LORA), tok(A_W), tok(A_W)),
        compiler_params=_cparams(("parallel", "parallel")),
        name="inproj",
    )(x, g.reshape(1, D_MODEL), shift, scale, w_in_bf16)


CONV_HALO = 128


def _conv_kernel(prev_ref, cur_ref, next_ref, w_ref, o_ref, e_ref, *, width, rows, tm):
    i = pl.program_id(1)
    last = pl.num_programs(1) - 1
    cb = o_ref.shape[-1]
    win = tm + 2 * CONV_HALO
    window = jnp.concatenate([
        jnp.where(i > 0, prev_ref[0].astype(F32), 0.0),
        cur_ref[0].astype(F32),
        jnp.where(i < last, next_ref[0].astype(F32), 0.0)], axis=0)
    spos = i * tm - CONV_HALO + lax.broadcasted_iota(jnp.int32, (win, cb), 0)
    col = spos & (width - 1)
    e_ref[0] = pltpu.roll(jnp.where(col == width - 1, 0.0, window), 1, 0)
    e_ref[1] = window
    e_ref[2] = pltpu.roll(jnp.where(col == 0, 0.0, window), win - 1, 0)
    acc = None
    for dr in (-1, 0, 1):
        if rows == 1 and dr != 0:
            continue
        for dc in (-1, 0, 1):
            xs = e_ref[dc + 1, pl.ds(CONV_HALO + dr * width, tm), :]
            term = xs * w_ref[dr + 1, dc + 1:dc + 2, :]
            acc = term if acc is None else acc + term
    o_ref[0] = acc


def _conv(z, w, width, tm, nch):
    b, t, _ = z.shape
    rows = t // width
    assert width & (width - 1) == 0 and (width + 1 <= CONV_HALO or rows == 1)
    cb = 512
    hb = tm // CONV_HALO
    nhalo = t // CONV_HALO
    kern = functools.partial(_conv_kernel, width=width, rows=rows, tm=tm)
    return pl.pallas_call(
        kern,
        out_shape=jax.ShapeDtypeStruct((b, t, nch), F32),
        grid=(b, t // tm, nch // cb),
        in_specs=[pl.BlockSpec((1, CONV_HALO, cb), lambda bi, i, c: (bi, jnp.maximum(i * hb - 1, 0), c)),
                  pl.BlockSpec((1, tm, cb), lambda bi, i, c: (bi, i, c)),
                  pl.BlockSpec((1, CONV_HALO, cb), lambda bi, i, c: (bi, jnp.minimum((i + 1) * hb, nhalo - 1), c)),
                  pl.BlockSpec((3, 3, cb), lambda bi, i, c: (0, 0, c))],
        out_specs=pl.BlockSpec((1, tm, cb), lambda bi, i, c: (bi, i, c)),
        scratch_shapes=[pltpu.VMEM((3, tm + 2 * CONV_HALO, cb), F32)],
        compiler_params=_cparams(("parallel", "parallel", "parallel")),
        name="conv",
    )(z, z, z, w)


def _prep_kernel(k_ref, v_ref, r_ref, lora_ref, gd_ref, kk_w_ref, w0_ref, wdec_ref, a0_ref, wicl_ref,
                 ka_ref, rk_ref, wgate_ref, bd_ref, dmat_ref, smat_ref,
                 at_ref, bt_ref, kt_ref, rt_ref, vo_ref, vec_ref, bv_ref, gate_ref):
    k = k_ref[0]
    v = v_ref[0]
    r = r_ref[0]
    lora = lora_ref[0]
    bd = bd_ref[...]
    kk = k * kk_w_ref[...]
    ss = _split_dot(kk * kk, bd)
    kk = kk * lax.rsqrt(jnp.maximum(ss, 1e-24))
    tl = jnp.tanh(lora[:, 0:2 * DECAY_LORA]).astype(BF16)
    la = lora[:, 2 * DECAY_LORA:].astype(BF16)
    kmod_sum = jnp.zeros_like(k)
    for d in range(2):
        logw = -math.exp(-0.5) * jax.nn.sigmoid(w0_ref[d:d + 1, :] + _dot(tl, wdec_ref[d]))
        a = jax.nn.sigmoid(a0_ref[d:d + 1, :] + _dot(la, wicl_ref[d]))
        kmod = k * (1.0 + (a - 1.0) * ka_ref[d:d + 1, :])
        kmod_sum = kmod_sum + kmod
        bvec = -(kk * a)
        rel = _split_dot_lhs(dmat_ref[d], logw)
        e_pos = jnp.exp(rel)
        e_neg = jnp.exp(-rel)
        at_ref[0, d] = (kk * jnp.exp(rel - logw)).astype(BF16)
        bt_ref[0, d] = (bvec * e_neg).astype(BF16)
        kt_ref[0, d] = (kmod * e_neg).astype(BF16)
        rt_ref[0, d] = (r * e_pos).astype(BF16)
        vec_ref[0, d, 0] = jnp.exp(_split_dot_lhs(smat_ref[d], logw))
    vo_ref[0] = v.astype(BF16)
    bonus = _split_dot(r * kmod_sum * rk_ref[...], bd)
    bv_ref[0] = bonus * v
    gate_ref[0] = _dot(jax.nn.sigmoid(gd_ref[0]).astype(BF16), wgate_ref[...])


def _split_dot_lhs(m_bf16, x):
    hi = x.astype(BF16)
    lo = (x - hi.astype(F32)).astype(BF16)
    return _dot(m_bf16, hi) + _dot(m_bf16, lo)


def _chunk_matrices(tm, chunk):
    t = np.arange(tm)
    same = (t[:, None] // chunk) == (t[None, :] // chunk)
    start = (t // chunk) * chunk
    half = chunk // 2
    d_fwd = same * ((t[None, :] <= t[:, None]).astype(np.float32)
                    - (t[None, :] < (start + half)[:, None]).astype(np.float32))
    d_rev = same * ((t[None, :] >= t[:, None]).astype(np.float32)
                    - (t[None, :] >= (start + half)[:, None]).astype(np.float32))
    nct = tm // chunk
    s_fwd = np.zeros((nct * 8, tm), np.float32)
    s_rev = np.zeros((nct * 8, tm), np.float32)
    for j in range(nct):
        in_chunk = (t // chunk) == j
        first = in_chunk & (t < j * chunk + half)
        second = in_chunk & (t >= j * chunk + half)
        s_fwd[8 * j + 0] = first
        s_fwd[8 * j + 1] = in_chunk
        s_fwd[8 * j + 2] = second
        s_rev[8 * j + 0] = second
        s_rev[8 * j + 1] = in_chunk
        s_rev[8 * j + 2] = first
    dmat = np.stack([d_fwd, d_rev]).astype(np.float32)
    smat = np.stack([s_fwd, s_rev])
    return jnp.asarray(dmat, BF16), jnp.asarray(smat, BF16)


def _prep(kvr, lora, gd, p, tm, chunk):
    b, t, _ = kvr.shape
    nct = tm // chunk
    dmat, smat = _chunk_matrices(tm, chunk)
    tokc = lambda c: pl.BlockSpec((1, tm, B_W), lambda bi, i: (bi, i, c))
    tok = lambda width: pl.BlockSpec((1, tm, width), lambda bi, i: (bi, i, 0))
    full = lambda shape: pl.BlockSpec(shape, lambda bi, i: (0,) * len(shape))
    dir_tok = pl.BlockSpec((1, 2, tm, B_W), lambda bi, i: (bi, 0, i, 0))
    dir_shape = jax.ShapeDtypeStruct((b, 2, t, B_W), BF16)
    outs = pl.pallas_call(
        _prep_kernel,
        out_shape=(dir_shape, dir_shape, dir_shape, dir_shape,
                   jax.ShapeDtypeStruct((b, t, B_W), BF16),
                   jax.ShapeDtypeStruct((b, 2, t // tm, nct * 8, B_W), F32),
                   jax.ShapeDtypeStruct((b, t, B_W), F32),
                   jax.ShapeDtypeStruct((b, t, B_W), F32)),
        grid=(b, t // tm),
        in_specs=[tokc(0), tokc(1), tokc(2), tok(OFF_GD - OFF_WD), tok(GATE_LORA),
                  full((1, B_W)), full((2, B_W)), full((2, 2 * DECAY_LORA, B_W)),
                  full((2, B_W)), full((2, 2 * AAA_LORA, B_W)), full((2, B_W)), full((1, B_W)),
                  full((GATE_LORA, B_W)), full((B_W, B_W)),
                  full((2, tm, tm)), full((2, nct * 8, tm))],
        out_specs=(dir_tok, dir_tok, dir_tok, dir_tok, tok(B_W),
                   pl.BlockSpec((1, 2, 1, nct * 8, B_W), lambda bi, i: (bi, 0, i, 0, 0)),
                   tok(B_W), tok(B_W)),
        compiler_params=_cparams(("parallel", "parallel")),
        name="prep",
    )(kvr, kvr, kvr, lora, gd, p["k_k"], p["decay_w0"], p["wdec"], p["iclr_a0"], p["wicl"],
      p["k_a"], p["r_k"], p["wgate"], p["bd"], dmat, smat)
    at, bt, kt, rt, vo, vec, bv, gate = outs
    vec = vec.reshape(b, 2, t // chunk, 8, B_W)
    return at, bt, kt, rt, vo, vec, bv, gate


def _rwkv_kernel(at0, bt0, kt0, rt0, v0, vec0, at1, bt1, kt1, rt1, v1, vec1, s0_ref,
                 y0_ref, y1_ref, sfin_ref, z_ref, *, chunk, nb):
    i = pl.program_id(1)
    L = chunk
    assert L & (L - 1) == 0
    a_refs, b_refs, k_refs, r_refs = (at0, at1), (bt0, bt1), (kt0, kt1), (rt0, rt1)
    v_refs, vec_refs, y_refs = (v0, v1), (vec0, vec1), (y0_ref, y1_ref)

    @pl.when(i == 0)
    def _():
        z_ref[...] = s0_ref[...]

    rowi = lax.broadcasted_iota(jnp.int32, (L, L), 0)
    coli = lax.broadcasted_iota(jnp.int32, (L, L), 1)
    strict = (rowi > coli, rowi < coli)
    incl = (rowi >= coli, rowi <= coli)
    eye_f = (rowi == coli).astype(F32)
    blk_xor = rowi ^ coli
    lane = lax.broadcasted_iota(jnp.int32, (L, LANES), 1)
    head0 = lane < B_HEAD
    sub = lax.broadcasted_iota(jnp.int32, (LANES, LANES), 0)
    lan = lax.broadcasted_iota(jnp.int32, (LANES, LANES), 1)
    same_head = (sub < B_HEAD) == (lan < B_HEAD)
    eye_c = sub == lan

    def split_heads(x):
        zero = jnp.zeros_like(x)
        return jnp.where(head0, x, zero), jnp.where(head0, zero, x)

    def stack_heads(x):
        x0, x1 = split_heads(x)
        return jnp.concatenate([x0, x1], axis=0)

    sls = [slice(p * LANES, (p + 1) * LANES) for p in range(N_PAIRS)]
    dps = [(n, d, p) for n in range(nb) for d in (0, 1) for p in range(N_PAIRS)]
    a_t = {(n, d, p): a_refs[d][n, 0, :, sls[p]] for n, d, p in dps}
    b_t = {(n, d, p): b_refs[d][n, 0, :, sls[p]] for n, d, p in dps}
    k_t = {(n, d, p): k_refs[d][n, 0, :, sls[p]] for n, d, p in dps}
    r_t = {(n, d, p): r_refs[d][n, 0, :, sls[p]] for n, d, p in dps}
    vv = {(n, d, p): v_refs[d][n, :, sls[p]] for n, d, p in dps}
    em = {(n, d, p): vec_refs[d][n, 0, 0, 0:1, sls[p]] for n, d, p in dps}
    pl_row = {(n, d, p): vec_refs[d][n, 0, 0, 1:2, sls[p]] for n, d, p in dps}
    epl = {(n, d, p): vec_refs[d][n, 0, 0, 2:3, sls[p]] for n, d, p in dps}
    z0 = {dp: z_ref[dp] for dp in dps}
    z0b = {dp: z0[dp].astype(BF16) for dp in dps}
    a_h = {dp: split_heads(a_t[dp]) for dp in dps}
    r_h = {dp: split_heads(r_t[dp]) for dp in dps}

    heads = [dp + (h,) for dp in dps for h in range(2)]
    tt, mf, mak, arb, ark = {}, {}, {}, {}, {}
    for ph in heads:
        dp, d, h = ph[:3], ph[1], ph[3]
        bk = jnp.concatenate([b_t[dp], k_t[dp]], axis=0)
        res = _dot_nt(jnp.concatenate([a_h[dp][h], r_h[dp][h]], axis=0), bk)
        mab = jnp.where(strict[d], res[0:L, 0:L], 0.0)
        mak[ph] = jnp.where(strict[d], res[0:L, L:2 * L], 0.0).astype(BF16)
        arb[ph] = jnp.where(incl[d], res[L:2 * L, 0:L], 0.0).astype(BF16)
        ark[ph] = jnp.where(incl[d], res[L:2 * L, L:2 * L], 0.0).astype(BF16)
        tt[ph] = eye_f + jnp.where(blk_xor == 1, mab, 0.0)
        mf[ph] = mab
    def take(x, s, d):
        if s % SUBLANES:
            return x
        first = 1 - d
        return jnp.concatenate([x[(2 * j + first) * s:(2 * j + first + 1) * s] for j in range(L // (2 * s))],
                               axis=0)

    def put(x, s, d):
        if s % SUBLANES:
            return x
        zero = jnp.zeros((s, x.shape[1]), x.dtype)
        pieces = []
        for j in range(L // (2 * s)):
            blk = x[j * s:(j + 1) * s]
            pieces += [zero, blk] if d == 0 else [blk, zero]
        return jnp.concatenate(pieces, axis=0)

    s = 2
    while s < L:
        level = (blk_xor >= s) & (blk_xor < 2 * s)
        xx = {ph: _dot(take(mf[ph], s, ph[1]).astype(BF16), tt[ph].astype(BF16)) for ph in heads}
        for ph in heads:
            w = _dot(take(tt[ph], s, ph[1]).astype(BF16), put(xx[ph], s, ph[1]).astype(BF16))
            tt[ph] = tt[ph] + jnp.where(level, put(w, s, ph[1]), 0.0)
        s *= 2

    v_stack = {dp: stack_heads(vv[dp]) for dp in dps}
    mv = {dp: _dot(jnp.concatenate([mak[dp + (0,)], mak[dp + (1,)]], axis=1), v_stack[dp]) for dp in dps}
    wu = {}
    for dp in dps:
        x_rhs = jnp.concatenate([jnp.concatenate(list(a_h[dp]), axis=0),
                                 stack_heads(mv[dp].astype(BF16))], axis=1)
        t_cat = jnp.concatenate([tt[dp + (0,)].astype(BF16), tt[dp + (1,)].astype(BF16)], axis=1)
        wu[dp] = _dot(t_cat, x_rhs)
    u = {dp: _dot((wu[dp][:, 0:LANES] * em[dp]).astype(BF16), z0b[dp]) + wu[dp][:, LANES:2 * LANES]
         for dp in dps}
    ub = {dp: u[dp].astype(BF16) for dp in dps}
    for dp in dps:
        n, d, p = dp
        y = (_dot((r_t[dp].astype(F32) * em[dp]).astype(BF16), z0b[dp])
             + _dot(jnp.concatenate([arb[dp + (0,)], arb[dp + (1,)]], axis=1), stack_heads(ub[dp]))
             + _dot(jnp.concatenate([ark[dp + (0,)], ark[dp + (1,)]], axis=1), v_stack[dp]))
        y_refs[d][n, :, sls[p]] = y
    for dp in dps:
        bhat_t = jnp.transpose(b_t[dp].astype(F32) * epl[dp]).astype(BF16)
        khat_t = jnp.transpose(k_t[dp].astype(F32) * epl[dp]).astype(BF16)
        upd = _dot(jnp.concatenate([bhat_t, khat_t], axis=1), jnp.concatenate([ub[dp], vv[dp]], axis=0))
        pl_col = jnp.sum(jnp.where(eye_c, pl_row[dp], 0.0), axis=1, keepdims=True)
        z_ref[dp] = pl_col * z0[dp] + jnp.where(same_head, upd, 0.0)

    sfin_ref[...] = z_ref[...]


def _rwkv(at, bt, kt, rt, vo, vec, s0, chunk):
    b, _, t, _ = at.shape
    nc = t // chunk
    nb = RW_NB if b % RW_NB == 0 else 1
    fwd_tok = pl.BlockSpec((nb, 1, chunk, B_W), lambda bi, i: (bi, 0, i, 0))
    rev_tok = pl.BlockSpec((nb, 1, chunk, B_W), lambda bi, i: (bi, 1, nc - 1 - i, 0))
    fwd_v = pl.BlockSpec((nb, chunk, B_W), lambda bi, i: (bi, i, 0))
    rev_v = pl.BlockSpec((nb, chunk, B_W), lambda bi, i: (bi, nc - 1 - i, 0))
    fwd_vec = pl.BlockSpec((nb, 1, 1, 8, B_W), lambda bi, i: (bi, 0, i, 0, 0))
    rev_vec = pl.BlockSpec((nb, 1, 1, 8, B_W), lambda bi, i: (bi, 1, nc - 1 - i, 0, 0))
    state = pl.BlockSpec((nb, 2, N_PAIRS, LANES, LANES), lambda bi, i: (bi, 0, 0, 0, 0))
    kern = functools.partial(_rwkv_kernel, chunk=chunk, nb=nb)
    return pl.pallas_call(
        kern,
        out_shape=(jax.ShapeDtypeStruct((b, t, B_W), F32),
                   jax.ShapeDtypeStruct((b, t, B_W), F32),
                   jax.ShapeDtypeStruct((b, 2, N_PAIRS, LANES, LANES), F32)),
        grid=(b // nb, nc),
        in_specs=[fwd_tok, fwd_tok, fwd_tok, fwd_tok, fwd_v, fwd_vec,
                  rev_tok, rev_tok, rev_tok, rev_tok, rev_v, rev_vec, state],
        out_specs=(fwd_v, rev_v, state),
        scratch_shapes=[pltpu.VMEM((nb, 2, N_PAIRS, LANES, LANES), F32)],
        compiler_params=_cparams(("parallel", "arbitrary")),
        name="rwkv",
    )(at, bt, kt, rt, vo, vec, at, bt, kt, rt, vo, vec, s0)


def _post_kernel(y0_ref, y1_ref, bv_ref, gate_ref, u_ref, va_ref, x_ref, mod_ref,
                 lnxg_ref, lnxb_ref, gmg_ref, gmb_ref, wsp_ref, bsp_ref, bd_ref,
                 wout_ref, gpost_ref, gpre_ref, wr_ref, br_ref,
                 x1_ref, h2_ref, comb_ref, *, tm):
    bdm = bd_ref[...]
    inv = 1.0 / B_HEAD

    def group_norm(val, eps):
        mu = _split_dot(val, bdm) * inv
        cen = val - mu
        var = _dot((cen * cen).astype(BF16), bdm) * inv
        return cen * lax.rsqrt(var + eps)

    y = y0_ref[0] + y1_ref[0]
    yb = group_norm(y, GN_EPS) * lnxg_ref[...] + lnxb_ref[...]
    out_b = (yb + bv_ref[0]) * gate_ref[0]

    uh = jax.nn.gelu(u_ref[0].astype(F32))
    vg = jax.nn.gelu(va_ref[0].astype(F32))
    vn = (group_norm(vg, LN_EPS) * gmg_ref[...] + gmb_ref[...]).astype(BF16)
    lane = lax.broadcasted_iota(jnp.int32, (GM_CHUNK, LANES), 1)
    head0 = lane < B_HEAD
    s_rows = []
    for c in range(tm // GM_CHUNK):
        cols = []
        for p in range(A_W // LANES):
            blk = vn[c * GM_CHUNK:(c + 1) * GM_CHUNK, p * LANES:(p + 1) * LANES]
            zero = jnp.zeros_like(blk)
            stack = jnp.concatenate([jnp.where(head0, blk, zero), jnp.where(head0, zero, blk)], axis=0)
            cols.append(_dot(wsp_ref[p], stack))
        s_rows.append(jnp.concatenate(cols, axis=1) + bsp_ref[...])
    s = jnp.concatenate(s_rows, axis=0)
    out_a = uh * s

    ymix = (_dot(out_a.astype(BF16), wout_ref[0:A_W, :]) + _dot(out_b.astype(BF16), wout_ref[A_W:, :]))
    gate1 = mod_ref[0, 0:1, :]
    shift2 = mod_ref[0, 1:2, :]
    scale2 = mod_ref[0, 2:3, :]
    ms = jnp.mean(ymix * ymix, axis=-1, keepdims=True)
    x1 = x_ref[0] + gate1 * (ymix * lax.rsqrt(ms + NORM_EPS) * gpost_ref[...])
    x1_ref[0] = x1
    ms2 = jnp.mean(x1 * x1, axis=-1, keepdims=True)
    h2 = x1 * lax.rsqrt(ms2 + NORM_EPS) * gpre_ref[...]
    h2 = h2 * (1.0 + scale2) + shift2
    for c in range(ROW_CHUNKS):
        h2_ref[0, :, c, :] = h2[:, c * LANES:(c + 1) * LANES]

    h2_hi = h2.astype(BF16)
    h2_lo = (h2 - h2_hi.astype(F32)).astype(BF16)
    logits = (_dot(h2_hi, wr_ref[0]) + _dot(h2_lo, wr_ref[0]) + _dot(h2_hi, wr_ref[1])) + br_ref[...]
    lni = lax.broadcasted_iota(jnp.int32, logits.shape, 1)
    ln = lni.astype(F32)
    lgrp = (lni // EXPERTS_PER_GROUP).astype(F32)
    neg = jnp.float32(-jnp.inf)
    big = jnp.float32(LANES)
    gmask = (lni >= N_EXPERTS) & (lni < N_EXPERTS + N_GROUPS)
    gl = jnp.where(gmask, logits, neg)
    gmax = jnp.max(gl, axis=-1, keepdims=True)
    gsum = jnp.sum(jnp.where(gmask, jnp.exp(gl - gmax), 0.0), axis=-1, keepdims=True)
    g_w = 1.0 / gsum
    g_i = jnp.min(jnp.where(gl == gmax, ln - N_EXPERTS, big), axis=-1, keepdims=True)
    emask = (lni < N_EXPERTS) & (lgrp == g_i)
    el = jnp.where(emask, logits, neg)
    l1 = jnp.max(el, axis=-1, keepdims=True)
    i1 = jnp.min(jnp.where(el == l1, ln, big), axis=-1, keepdims=True)
    el2 = jnp.where(ln == i1, neg, el)
    l2 = jnp.max(el2, axis=-1, keepdims=True)
    i2 = jnp.min(jnp.where(el2 == l2, ln, big), axis=-1, keepdims=True)
    e21 = jnp.exp(l2 - l1)
    w1 = 1.0 / (1.0 + e21)
    w2 = e21 * w1
    comb_ref[0] = (g_w * (jnp.where(ln == i1, w1, 0.0) + jnp.where(ln == i2, w2, 0.0))
                   + jnp.where((ln == i1 + MEMB) | (ln == i2 + MEMB), 1.0, 0.0))


def _post(y0, y1, bv, gate, u, va, x, mod3, p, tm):
    b, t, _ = x.shape
    tok = lambda width: pl.BlockSpec((1, tm, width), lambda bi, i: (bi, i, 0))
    full = lambda shape: pl.BlockSpec(shape, lambda bi, i: (0,) * len(shape))
    kern = functools.partial(_post_kernel, tm=tm)
    return pl.pallas_call(
        kern,
        out_shape=(jax.ShapeDtypeStruct((b, t, D_MODEL), F32),
                   jax.ShapeDtypeStruct((b, t, ROW_CHUNKS, LANES), F32),
                   jax.ShapeDtypeStruct((b, t, LANES), F32)),
        grid=(b, t // tm),
        in_specs=[tok(B_W), tok(B_W), tok(B_W), tok(B_W), tok(A_W), tok(A_W), tok(D_MODEL),
                  pl.BlockSpec((1, 8, D_MODEL), lambda bi, i: (bi, 0, 0)),
                  full((1, B_W)), full((1, B_W)), full((1, A_W)), full((1, A_W)),
                  full((A_W // LANES, GM_CHUNK, 2 * GM_CHUNK)), full((GM_CHUNK, A_W)), full((B_W, B_W)),
                  full((D_MODEL, D_MODEL)), full((1, D_MODEL)), full((1, D_MODEL)),
                  full((2, D_MODEL, LANES)), full((1, LANES))],
        out_specs=(tok(D_MODEL), pl.BlockSpec((1, tm, ROW_CHUNKS, LANES), lambda bi, i: (bi, i, 0, 0)),
                   tok(LANES)),
        compiler_params=_cparams(("parallel", "parallel")),
        name="post",
    )(y0, y1, bv, gate, u, va, x, mod3, p["lnx_g"], p["lnx_b"], p["gm_ln_g"], p["gm_ln_b"],
      p["wsp"], p["bsp"], p["bd"], p["w_out"], p["g_post1"], p["g_pre2"], p["w_router"], p["b_router"])


def _split3(x):
    hi = x.astype(BF16)
    r1 = x - hi.astype(F32)
    mid = r1.astype(BF16)
    lo = (r1 - mid.astype(F32)).astype(BF16)
    return hi, mid, lo


def _route_kernel(comb_ref, tri_ref, upper_ref, dst_ref, meta_ref, cnt_ref, carry_ref, offs_ref, ends_ref,
                  *, tm, tr, ntp):
    ph = pl.program_id(0)
    i = pl.program_id(1)
    last = pl.num_programs(1) - 1
    comb = comb_ref[...]
    lni = lax.broadcasted_iota(jnp.int32, (tm, LANES), 1)
    member = jnp.where((lni >= MEMB) & (lni < MEMB + N_EXPERTS), comb, 0.0)
    colsum = jnp.sum(member, axis=0, keepdims=True)

    @pl.when((ph == 0) & (i == 0))
    def _():
        cnt_ref[...] = jnp.zeros_like(cnt_ref)

    @pl.when(ph == 0)
    def _():
        cnt_ref[...] += colsum

    @pl.when((ph == 1) & (i == 0))
    def _():
        cnt = cnt_ref[...]
        padded = jnp.floor((cnt + (tr - 1)) * (1.0 / tr)) * tr
        hi, mid, lo = _split3(padded)
        up = upper_ref[...]
        ends = _dot(hi, up) + _dot(mid, up) + _dot(lo, up)
        ends_ref[...] = ends
        offs_ref[...] = ends - padded
        carry_ref[...] = jnp.zeros_like(carry_ref)

    @pl.when(ph == 1)
    def _():
        cum = _dot(tri_ref[...], member.astype(BF16))
        pos = offs_ref[0:1, :] + carry_ref[0:1, :] + cum
        lnf = lni.astype(F32)
        ea = jnp.min(jnp.where(member > 0, lnf, 2.0 * LANES), axis=-1, keepdims=True)
        eb = jnp.max(jnp.where(member > 0, lnf, -1.0), axis=-1, keepdims=True)
        ones8 = jnp.ones((8, LANES), BF16)
        for slot, e_sel in enumerate((ea, eb)):
            hi, mid, lo = _split3(jnp.where(lnf == e_sel, pos, 0.0))
            row = _dot_nt(ones8, hi) + _dot_nt(ones8, mid) + _dot_nt(ones8, lo)
            dst_ref[0, :, slot * tm:(slot + 1) * tm] = row.astype(jnp.int32)
        carry_ref[...] += colsum

    @pl.when((ph == 1) & (i == last))
    def _():
        ends = ends_ref[...]
        r = lax.broadcasted_iota(jnp.int32, (LANES, LANES), 0)
        c = lax.broadcasted_iota(jnp.int32, (LANES, LANES), 1)
        ends_col = jnp.sum(jnp.where(r == c, ends[0:1, :], 0.0), axis=1, keepdims=True)
        sub = lax.broadcasted_iota(jnp.int32, (LANES, ntp), 0)
        start = lax.broadcasted_iota(jnp.int32, (LANES, ntp), 1).astype(F32) * float(tr)
        is_exp = (sub >= MEMB) & (sub < MEMB + N_EXPERTS)
        done = jnp.where(is_exp & (ends_col <= start), 1.0, 0.0)
        tile_expert = jnp.minimum(jnp.sum(done, axis=0, keepdims=True), N_EXPERTS - 1.0)
        n_used = jnp.max(ends[0:1, :], axis=-1, keepdims=True) * (1.0 / tr)
        rowi = lax.broadcasted_iota(jnp.int32, (8, ntp), 0)
        meta_ref[...] = jnp.where(rowi == 0, tile_expert, n_used).astype(jnp.int32)


def _route(comb2d, tm, tr, ntp):
    n = comb2d.shape[0]
    nt = n // tm
    t = np.arange(tm)
    tri = jnp.asarray(t[None, :] < t[:, None], BF16)
    e = np.arange(LANES)
    upper = jnp.asarray(e[:, None] <= e[None, :], BF16)
    kern = functools.partial(_route_kernel, tm=tm, tr=tr, ntp=ntp)
    return pl.pallas_call(
        kern,
        out_shape=(jax.ShapeDtypeStruct((nt, 8, 2 * tm), jnp.int32),
                   jax.ShapeDtypeStruct((8, ntp), jnp.int32)),
        grid=(2, nt),
        in_specs=[pl.BlockSpec((tm, LANES), lambda ph, i: (i, 0)),
                  pl.BlockSpec((tm, tm), lambda ph, i: (0, 0)),
                  pl.BlockSpec((LANES, LANES), lambda ph, i: (0, 0))],
        out_specs=(pl.BlockSpec((1, 8, 2 * tm), lambda ph, i: (i * ph, 0, 0)),
                   pl.BlockSpec((8, ntp), lambda ph, i: (0, 0))),
        scratch_shapes=[pltpu.VMEM((8, LANES), F32)] * 4,
        compiler_params=_cparams(("arbitrary", "arbitrary")),
        name="route",
    )(comb2d, tri, upper)


def _dispatch_kernel(dst_ref, h_hbm, xs_in, xs_hbm, sem, *, tm):
    del xs_in
    i = pl.program_id(0)

    def issue(t, carry):
        src = h_hbm.at[i * tm + t]
        pltpu.make_async_copy(src, xs_hbm.at[dst_ref[0, 0, t]], sem.at[0]).start()
        pltpu.make_async_copy(src, xs_hbm.at[dst_ref[0, 0, tm + t]], sem.at[0]).start()
        return carry

    lax.fori_loop(0, tm, issue, 0)

    def drain(t, carry):
        pltpu.make_async_copy(h_hbm.at[0], xs_hbm.at[0], sem.at[0]).wait()
        pltpu.make_async_copy(h_hbm.at[0], xs_hbm.at[0], sem.at[0]).wait()
        return carry

    lax.fori_loop(0, tm, drain, 0)


def _dispatch(dst, h3, n_rows, tm):
    n = h3.shape[0]
    xs0 = jnp.zeros((n_rows, ROW_CHUNKS, LANES), F32)
    kern = functools.partial(_dispatch_kernel, tm=tm)
    return pl.pallas_call(
        kern,
        out_shape=jax.ShapeDtypeStruct((n_rows, ROW_CHUNKS, LANES), F32),
        grid=(n // tm,),
        in_specs=[pl.BlockSpec((1, 1, 2 * tm), lambda i: (i, 0, 0), memory_space=pltpu.SMEM),
                  pl.BlockSpec(memory_space=pl.ANY),
                  pl.BlockSpec(memory_space=pl.ANY)],
        out_specs=pl.BlockSpec(memory_space=pl.ANY),
        scratch_shapes=[pltpu.SemaphoreType.DMA((1,))],
        input_output_aliases={2: 0},
        compiler_params=_cparams(("arbitrary",)),
        name="dispatch",
    )(dst, h3, xs0)


def _gmm_kernel(meta_ref, xs_ref, wg_ref, wu_ref, wd_ref, ys_ref):
    j = pl.program_id(0)
    n_used = meta_ref[1, 0]

    @pl.when(j < n_used)
    def _():
        x = jnp.concatenate([xs_ref[:, c, :] for c in range(ROW_CHUNKS)], axis=1).astype(BF16)
        gate = _dot(x, wg_ref[0])
        up = _dot(x, wu_ref[0])
        hid = (gate * jax.nn.sigmoid(gate) * up).astype(BF16)
        y = _dot(hid, wd_ref[0])
        for c in range(ROW_CHUNKS):
            ys_ref[:, c, :] = y[:, c * LANES:(c + 1) * LANES]

    @pl.when(j >= n_used)
    def _():
        ys_ref[...] = jnp.zeros_like(ys_ref)


def _gmm(meta, xs, wg, wu, wd, tr):
    n_rows = xs.shape[0]
    row_spec = lambda fn: pl.BlockSpec((tr, ROW_CHUNKS, LANES), fn)
    return pl.pallas_call(
        _gmm_kernel,
        out_shape=jax.ShapeDtypeStruct(xs.shape, F32),
        grid_spec=pltpu.PrefetchScalarGridSpec(
            num_scalar_prefetch=1,
            grid=(n_rows // tr,),
            in_specs=[row_spec(lambda j, m: (jnp.minimum(j, m[1, 0] - 1), 0, 0)),
                      pl.BlockSpec((1, D_MODEL, D_EXPERT), lambda j, m: (m[0, j], 0, 0)),
                      pl.BlockSpec((1, D_MODEL, D_EXPERT), lambda j, m: (m[0, j], 0, 0)),
                      pl.BlockSpec((1, D_EXPERT, D_MODEL), lambda j, m: (m[0, j], 0, 0))],
            out_specs=row_spec(lambda j, m: (j, 0, 0))),
        compiler_params=_cparams(("arbitrary",)),
        name="gmm",
    )(meta, xs, wg, wu, wd)


def _combine_kernel(dst_ref, ys_hbm, comb_ref, x1_ref, mod_ref, gpost_ref, o_ref, buf_a, buf_b, sem, *, tm):
    def issue(t, carry):
        pltpu.make_async_copy(ys_hbm.at[dst_ref[0, 0, t]], buf_a.at[t], sem.at[0]).start()
        pltpu.make_async_copy(ys_hbm.at[dst_ref[0, 0, tm + t]], buf_b.at[t], sem.at[1]).start()
        return carry

    lax.fori_loop(0, tm, issue, 0)

    def drain(t, carry):
        pltpu.make_async_copy(ys_hbm.at[0], buf_a.at[t], sem.at[0]).wait()
        pltpu.make_async_copy(ys_hbm.at[0], buf_b.at[t], sem.at[1]).wait()
        return carry

    lax.fori_loop(0, tm, drain, 0)

    comb = comb_ref[0]
    lni = lax.broadcasted_iota(jnp.int32, comb.shape, 1)
    lnf = lni.astype(F32)
    member = (lni >= MEMB) & (lni < MEMB + N_EXPERTS) & (comb > 0)
    ea = jnp.min(jnp.where(member, lnf, 2.0 * LANES), axis=-1, keepdims=True) - MEMB
    eb = jnp.max(jnp.where(member, lnf, -1.0), axis=-1, keepdims=True) - MEMB
    wa = jnp.sum(jnp.where(lnf == ea, comb, 0.0), axis=-1, keepdims=True)
    wb = jnp.sum(jnp.where(lnf == eb, comb, 0.0), axis=-1, keepdims=True)
    ya = jnp.concatenate([buf_a[:, c, :] for c in range(ROW_CHUNKS)], axis=1)
    yb = jnp.concatenate([buf_b[:, c, :] for c in range(ROW_CHUNKS)], axis=1)
    y = wa * ya + wb * yb
    ms = jnp.mean(y * y, axis=-1, keepdims=True)
    o_ref[0] = x1_ref[0] + mod_ref[0, 3:4, :] * (y * lax.rsqrt(ms + NORM_EPS) * gpost_ref[...])


def _combine(dst, ys, comb, x1, mod3, g_post2, tm):
    b, t, _ = x1.shape
    nti = t // tm
    tok = lambda width: pl.BlockSpec((1, tm, width), lambda bi, i: (bi, i, 0))
    kern = functools.partial(_combine_kernel, tm=tm)
    return pl.pallas_call(
        kern,
        out_shape=jax.ShapeDtypeStruct((b, t, D_MODEL), F32),
        grid=(b, nti),
        in_specs=[pl.BlockSpec((1, 1, 2 * tm), lambda bi, i: (bi * nti + i, 0, 0), memory_space=pltpu.SMEM),
                  pl.BlockSpec(memory_space=pl.ANY),
                  tok(LANES), tok(D_MODEL),
                  pl.BlockSpec((1, 8, D_MODEL), lambda bi, i: (bi, 0, 0)),
                  pl.BlockSpec((1, D_MODEL), lambda bi, i: (0, 0))],
        out_specs=tok(D_MODEL),
        scratch_shapes=[pltpu.VMEM((tm, ROW_CHUNKS, LANES), F32), pltpu.VMEM((tm, ROW_CHUNKS, LANES), F32),
                        pltpu.SemaphoreType.DMA((2,))],
        compiler_params=_cparams(("arbitrary", "arbitrary")),
        name="combine",
    )(dst, ys, comb, x1, mod3, g_post2.reshape(1, D_MODEL))


def _moe(h3, comb, x1, mod3, wg, wu, wd, g_post2):
    b, t, _ = x1.shape
    n = b * t
    tm = _tile(t, MOE_TM)
    n_rows = 2 * n + N_EXPERTS * MOE_TR
    ntp = -(-(n_rows // MOE_TR) // LANES) * LANES
    dst8, meta = _route(comb.reshape(n, LANES), tm, MOE_TR, ntp)
    dst = dst8[:, 0:1, :]
    xs = _dispatch(dst, h3.reshape(n, ROW_CHUNKS, LANES), n_rows, tm)
    ys = _gmm(meta[0:2], xs, wg, wu, wd, MOE_TR)
    return _combine(dst, ys, comb, x1, mod3, g_post2, tm)


def _tile(t, pref):
    while t % pref:
        pref //= 2
    return pref


def kernel(x, c, ctx, c_ctx, w_mod, b_mod, g_pre1, g_post1, g_pre2, g_post2, w_in, conv_rkv, gm_ln_g, gm_ln_b, w_spatial, b_spatial, decay_w0, decay_up, iclr_a0, iclr_up, k_k, k_a, r_k, outgate_up, lnx_g, lnx_b, w_out, w_router_grp, b_router_grp, w_router_exp, b_router_exp, w_gate, w_up, w_down):
    assert w_mod.shape[0] == 1, "single-layer kernel"
    b, t, _ = x.shape
    tc = ctx.shape[1]
    assert b <= 7 and t % RW_CHUNK == 0 and tc % RW_CHUNK == 0 and t % GRID_W == 0

    head_of = np.arange(B_W) // B_HEAD
    bd = jnp.asarray(head_of[:, None] == head_of[None, :], BF16)
    zpad = jnp.zeros((2, DECAY_LORA, B_W), F32)
    wdec = jnp.stack([jnp.concatenate([decay_up[0, 0], zpad[0]], 0),
                      jnp.concatenate([zpad[0], decay_up[0, 1]], 0)]).astype(BF16)
    wicl = jnp.stack([jnp.concatenate([iclr_up[0, 0], zpad[0]], 0),
                      jnp.concatenate([zpad[0], iclr_up[0, 1]], 0)]).astype(BF16)
    wsp = w_spatial[0].reshape(A_W // LANES, 2, GM_CHUNK, GM_CHUNK)
    wsp = jnp.concatenate([wsp[:, 0], wsp[:, 1]], axis=-1).astype(BF16)
    bsp = jnp.repeat(b_spatial[0].T, A_W // A_GROUPS, axis=1)
    w_router = jnp.concatenate(
        [w_router_exp[0], w_router_grp[0],
         jnp.zeros((D_MODEL, LANES - N_EXPERTS - N_GROUPS), F32)], axis=1)
    w_router_hi = w_router.astype(BF16)
    w_router = jnp.stack([w_router_hi, (w_router - w_router_hi.astype(F32)).astype(BF16)])
    b_router = jnp.concatenate(
        [b_router_exp[0], b_router_grp[0], jnp.zeros((LANES - N_EXPERTS - N_GROUPS,), F32)]).reshape(1, LANES)
    prm = {
        "k_k": k_k[0].reshape(1, B_W), "decay_w0": decay_w0[0], "wdec": wdec,
        "iclr_a0": iclr_a0[0], "wicl": wicl, "k_a": k_a[0], "r_k": r_k[0].reshape(1, B_W),
        "wgate": outgate_up[0].astype(BF16), "bd": bd,
        "lnx_g": lnx_g[0].reshape(1, B_W), "lnx_b": lnx_b[0].reshape(1, B_W),
        "gm_ln_g": gm_ln_g[0].reshape(1, A_W), "gm_ln_b": gm_ln_b[0].reshape(1, A_W),
        "wsp": wsp, "bsp": bsp, "w_out": w_out[0].astype(BF16),
        "g_post1": g_post1[0].reshape(1, D_MODEL), "g_pre2": g_pre2[0].reshape(1, D_MODEL),
        "w_router": w_router, "b_router": b_router,
    }
    w_in_b = w_in[0].astype(BF16)

    cond8 = jnp.concatenate([c, c_ctx[None, :], jnp.zeros((8 - b - 1, D_MODEL), F32)], axis=0)
    mod = _mod(cond8, w_mod[0], b_mod[0])
    mod6 = mod.reshape(8, 6, D_MODEL)
    shift1, scale1 = mod6[:b, 0:1], mod6[:b, 1:2]
    cshift1 = jnp.broadcast_to(mod6[b:b + 1, 0:1], (b, 1, D_MODEL))
    cscale1 = jnp.broadcast_to(mod6[b:b + 1, 1:2], (b, 1, D_MODEL))
    mod3 = jnp.concatenate([mod6[:b, 2:6], jnp.zeros((b, 4, D_MODEL), F32)], axis=1)

    c_rkv, c_lora, c_gd, _, _ = _inproj(ctx, g_pre1[0], cshift1, cscale1, w_in_b, _tile(tc, 256))
    c_kvr = _conv(c_rkv, conv_rkv[0], tc, _tile(tc, 256), 3 * B_W)
    c_at, c_bt, c_kt, c_rt, c_v, c_vec, _, _ = _prep(c_kvr, c_lora, c_gd, prm, _tile(tc, 256), RW_CHUNK)
    s_zero = jnp.zeros((b, 2, N_PAIRS, LANES, LANES), F32)
    _, _, states = _rwkv(c_at, c_bt, c_kt, c_rt, c_v, c_vec, s_zero, RW_CHUNK)

    rkv, lora, gd, u, va = _inproj(x, g_pre1[0], shift1, scale1, w_in_b, _tile(t, 512))
    kvr = _conv(rkv, conv_rkv[0], GRID_W, _tile(t, 512), 3 * B_W)
    at, bt, kt, rt, vo, vec, bv, gate = _prep(kvr, lora, gd, prm, _tile(t, 256), RW_CHUNK)
    y0, y1, _ = _rwkv(at, bt, kt, rt, vo, vec, states, RW_CHUNK)
    x1, h2, comb = _post(y0, y1, bv, gate, u, va, x, mod3, prm, _tile(t, 256))

    return _moe(h2, comb, x1, mod3, w_gate[0].astype(BF16), w_up[0].astype(BF16), w_down[0].astype(BF16),
                g_post2[0])
```

```python
import functools
import math

import numpy as np
import jax
import jax.numpy as jnp
from jax import lax
from jax.experimental import pallas as pl
from jax.experimental.pallas import tpu as pltpu

D_MODEL = 1024
GRID_W = 64
A_W = 512
A_GROUPS = 8
GM_CHUNK = 128
B_W = 512
B_HEADS = 8
B_HEAD = 64
DECAY_LORA = 64
AAA_LORA = 64
GATE_LORA = 128
OFF_WD = 3 * B_W
OFF_GD = OFF_WD + 2 * DECAY_LORA + 2 * AAA_LORA
OFF_U = OFF_GD + GATE_LORA
OFF_VA = OFF_U + A_W
D_IN = OFF_VA + A_W
N_GROUPS = 4
EXPERTS_PER_GROUP = 8
N_EXPERTS = 32
D_EXPERT = 256
NORM_EPS = 1e-6
LN_EPS = 1e-5
GN_EPS = 64e-5

LANES = 128
SUBLANES = 8
RW_CHUNK = 128
RW_NB = 2
N_PAIRS = B_W // LANES
ROW_CHUNKS = D_MODEL // LANES
MOE_TM = 512
MOE_TR = 256
MEMB = 64
VMEM_LIMIT = 48 * 1024 * 1024

BF16 = jnp.bfloat16
F32 = jnp.float32


def _dot(a, b):
    return jnp.dot(a, b, preferred_element_type=F32)


def _dot_nt(a, b):
    return lax.dot_general(a, b, (((1,), (1,)), ((), ())), preferred_element_type=F32)


def _split_dot(x, w_bf16):
    hi = x.astype(BF16)
    lo = (x - hi.astype(F32)).astype(BF16)
    return _dot(hi, w_bf16) + _dot(lo, w_bf16)


def _cparams(sem):
    return pltpu.CompilerParams(dimension_semantics=sem, vmem_limit_bytes=VMEM_LIMIT)


def _mod_kernel(c_ref, w_ref, b_ref, o_ref):
    c = c_ref[...]
    s = c * jax.nn.sigmoid(c)
    o_ref[...] = jnp.dot(s, w_ref[...], preferred_element_type=F32,
                         precision=lax.Precision.HIGHEST) + b_ref[...]


def _mod(cond8, w_mod, b_mod):
    n = w_mod.shape[1]
    tn = 1024
    return pl.pallas_call(
        _mod_kernel,
        out_shape=jax.ShapeDtypeStruct((8, n), F32),
        grid=(n // tn,),
        in_specs=[pl.BlockSpec((8, D_MODEL), lambda j: (0, 0)),
                  pl.BlockSpec((D_MODEL, tn), lambda j: (0, j)),
                  pl.BlockSpec((1, tn), lambda j: (0, j))],
        out_specs=pl.BlockSpec((8, tn), lambda j: (0, j)),
        compiler_params=_cparams(("arbitrary",)),
        name="mod",
    )(cond8, w_mod, b_mod.reshape(1, n))


def _inproj_kernel(x_ref, g_ref, sh_ref, sc_ref, w_ref, rkv_ref, lora_ref, gd_ref, u_ref, va_ref):
    x = x_ref[0]
    ms = jnp.mean(x * x, axis=-1, keepdims=True)
    h = x * lax.rsqrt(ms + NORM_EPS) * g_ref[...]
    h = h * (1.0 + sc_ref[0]) + sh_ref[0]
    hb = h.astype(BF16)
    rkv_ref[0] = _dot(hb, w_ref[:, 0:OFF_WD]).astype(rkv_ref.dtype)
    lora_ref[0] = _dot(hb, w_ref[:, OFF_WD:OFF_GD])
    gd_ref[0] = _dot(hb, w_ref[:, OFF_GD:OFF_U])
    u_ref[0] = _dot(hb, w_ref[:, OFF_U:OFF_VA]).astype(u_ref.dtype)
    va_ref[0] = _dot(hb, w_ref[:, OFF_VA:D_IN]).astype(va_ref.dtype)


def _inproj(x, g, shift, scale, w_in_bf16, tm):
    b, t, _ = x.shape
    tok = lambda width: pl.BlockSpec((1, tm, width), lambda bi, i: (bi, i, 0))
    vec = pl.BlockSpec((1, 1, D_MODEL), lambda bi, i: (bi, 0, 0))
    return pl.pallas_call(
        _inproj_kernel,
        out_shape=(jax.ShapeDtypeStruct((b, t, OFF_WD), BF16),
                   jax.ShapeDtypeStruct((b, t, OFF_GD - OFF_WD), F32),
                   jax.ShapeDtypeStruct((b, t, GATE_LORA), F32),
                   jax.ShapeDtypeStruct((b, t, A_W), BF16),
                   jax.ShapeDtypeStruct((b, t, A_W), BF16)),
        grid=(b, t // tm),
        in_specs=[tok(D_MODEL),
                  pl.BlockSpec((1, D_MODEL), lambda bi, i: (0, 0)),
                  vec, vec,
                  pl.BlockSpec((D_MODEL, D_IN), lambda bi, i: (0, 0))],
        out_specs=(tok(OFF_WD), tok(OFF_GD - OFF_WD), tok(GATE_LORA), tok(A_W), tok(A_W)),
        compiler_params=_cparams(("parallel", "parallel")),
        name="inproj",
    )(x, g.reshape(1, D_MODEL), shift, scale, w_in_bf16)


CONV_HALO = 128


def _conv_kernel(prev_ref, cur_ref, next_ref, w_ref, o_ref, e_ref, *, width, rows, tm):
    i = pl.program_id(1)
    last = pl.num_programs(1) - 1
    cb = o_ref.shape[-1]
    win = tm + 2 * CONV_HALO
    window = jnp.concatenate([
        jnp.where(i > 0, prev_ref[0].astype(F32), 0.0),
        cur_ref[0].astype(F32),
        jnp.where(i < last, next_ref[0].astype(F32), 0.0)], axis=0)
    spos = i * tm - CONV_HALO + lax.broadcasted_iota(jnp.int32, (win, cb), 0)
    col = spos & (width - 1)
    e_ref[0] = pltpu.roll(jnp.where(col == width - 1, 0.0, window), 1, 0)
    e_ref[1] = window
    e_ref[2] = pltpu.roll(jnp.where(col == 0, 0.0, window), win - 1, 0)
    acc = None
    for dr in (-1, 0, 1):
        if rows == 1 and dr != 0:
            continue
        for dc in (-1, 0, 1):
            xs = e_ref[dc + 1, pl.ds(CONV_HALO + dr * width, tm), :]
            term = xs * w_ref[dr + 1, dc + 1:dc + 2, :]
            acc = term if acc is None else acc + term
    o_ref[0] = acc


def _conv(z, w, width, tm, nch):
    b, t, _ = z.shape
    rows = t // width
    assert width & (width - 1) == 0 and (width + 1 <= CONV_HALO or rows == 1)
    cb = 512
    hb = tm // CONV_HALO
    nhalo = t // CONV_HALO
    kern = functools.partial(_conv_kernel, width=width, rows=rows, tm=tm)
    return pl.pallas_call(
        kern,
        out_shape=jax.ShapeDtypeStruct((b, t, nch), F32),
        grid=(b, t // tm, nch // cb),
        in_specs=[pl.BlockSpec((1, CONV_HALO, cb), lambda bi, i, c: (bi, jnp.maximum(i * hb - 1, 0), c)),
                  pl.BlockSpec((1, tm, cb), lambda bi, i, c: (bi, i, c)),
                  pl.BlockSpec((1, CONV_HALO, cb), lambda bi, i, c: (bi, jnp.minimum((i + 1) * hb, nhalo - 1), c)),
                  pl.BlockSpec((3, 3, cb), lambda bi, i, c: (0, 0, c))],
        out_specs=pl.BlockSpec((1, tm, cb), lambda bi, i, c: (bi, i, c)),
        scratch_shapes=[pltpu.VMEM((3, tm + 2 * CONV_HALO, cb), F32)],
        compiler_params=_cparams(("parallel", "parallel", "parallel")),
        name="conv",
    )(z, z, z, w)


def _prep_kernel(k_ref, v_ref, r_ref, lora_ref, gd_ref, kk_w_ref, w0_ref, wdec_ref, a0_ref, wicl_ref,
                 ka_ref, rk_ref, wgate_ref, bd_ref, dmat_ref, smat_ref,
                 at_ref, bt_ref, kt_ref, rt_ref, vo_ref, vec_ref, bv_ref, gate_ref):
    k = k_ref[0]
    v = v_ref[0]
    r = r_ref[0]
    lora = lora_ref[0]
    bd = bd_ref[...]
    kk = k * kk_w_ref[...]
    ss = _split_dot(kk * kk, bd)
    kk = kk * lax.rsqrt(jnp.maximum(ss, 1e-24))
    tl = jnp.tanh(lora[:, 0:2 * DECAY_LORA]).astype(BF16)
    la = lora[:, 2 * DECAY_LORA:].astype(BF16)
    kmod_sum = jnp.zeros_like(k)
    for d in range(2):
        logw = -math.exp(-0.5) * jax.nn.sigmoid(w0_ref[d:d + 1, :] + _dot(tl, wdec_ref[d]))
        a = jax.nn.sigmoid(a0_ref[d:d + 1, :] + _dot(la, wicl_ref[d]))
        kmod = k * (1.0 + (a - 1.0) * ka_ref[d:d + 1, :])
        kmod_sum = kmod_sum + kmod
        bvec = -(kk * a)
        rel = _split_dot_lhs(dmat_ref[d], logw)
        e_pos = jnp.exp(rel)
        e_neg = jnp.exp(-rel)
        at_ref[0, d] = (kk * jnp.exp(rel - logw)).astype(BF16)
        bt_ref[0, d] = (bvec * e_neg).astype(BF16)
        kt_ref[0, d] = (kmod * e_neg).astype(BF16)
        rt_ref[0, d] = (r * e_pos).astype(BF16)
        vec_ref[0, d, 0] = jnp.exp(_split_dot_lhs(smat_ref[d], logw))
    vo_ref[0] = v.astype(BF16)
    bonus = _split_dot(r * kmod_sum * rk_ref[...], bd)
    bv_ref[0] = bonus * v
    gate_ref[0] = _dot(jax.nn.sigmoid(gd_ref[0]).astype(BF16), wgate_ref[...])


def _split_dot_lhs(m_bf16, x):
    hi = x.astype(BF16)
    lo = (x - hi.astype(F32)).astype(BF16)
    return _dot(m_bf16, hi) + _dot(m_bf16, lo)


def _chunk_matrices(tm, chunk):
    t = np.arange(tm)
    same = (t[:, None] // chunk) == (t[None, :] // chunk)
    start = (t // chunk) * chunk
    half = chunk // 2
    d_fwd = same * ((t[None, :] <= t[:, None]).astype(np.float32)
                    - (t[None, :] < (start + half)[:, None]).astype(np.float32))
    d_rev = same * ((t[None, :] >= t[:, None]).astype(np.float32)
                    - (t[None, :] >= (start + half)[:, None]).astype(np.float32))
    nct = tm // chunk
    s_fwd = np.zeros((nct * 8, tm), np.float32)
    s_rev = np.zeros((nct * 8, tm), np.float32)
    for j in range(nct):
        in_chunk = (t // chunk) == j
        first = in_chunk & (t < j * chunk + half)
        second = in_chunk & (t >= j * chunk + half)
        s_fwd[8 * j + 0] = first
        s_fwd[8 * j + 1] = in_chunk
        s_fwd[8 * j + 2] = second
        s_rev[8 * j + 0] = second
        s_rev[8 * j + 1] = in_chunk
        s_rev[8 * j + 2] = first
    dmat = np.stack([d_fwd, d_rev]).astype(np.float32)
    smat = np.stack([s_fwd, s_rev])
    return jnp.asarray(dmat, BF16), jnp.asarray(smat, BF16)


def _prep(kvr, lora, gd, p, tm, chunk):
    b, t, _ = kvr.shape
    nct = tm // chunk
    dmat, smat = _chunk_matrices(tm, chunk)
    tokc = lambda c: pl.BlockSpec((1, tm, B_W), lambda bi, i: (bi, i, c))
    tok = lambda width: pl.BlockSpec((1, tm, width), lambda bi, i: (bi, i, 0))
    full = lambda shape: pl.BlockSpec(shape, lambda bi, i: (0,) * len(shape))
    dir_tok = pl.BlockSpec((1, 2, tm, B_W), lambda bi, i: (bi, 0, i, 0))
    dir_shape = jax.ShapeDtypeStruct((b, 2, t, B_W), BF16)
    outs = pl.pallas_call(
        _prep_kernel,
        out_shape=(dir_shape, dir_shape, dir_shape, dir_shape,
                   jax.ShapeDtypeStruct((b, t, B_W), BF16),
                   jax.ShapeDtypeStruct((b, 2, t // tm, nct * 8, B_W), F32),
                   jax.ShapeDtypeStruct((b, t, B_W), F32),
                   jax.ShapeDtypeStruct((b, t, B_W), F32)),
        grid=(b, t // tm),
        in_specs=[tokc(0), tokc(1), tokc(2), tok(OFF_GD - OFF_WD), tok(GATE_LORA),
                  full((1, B_W)), full((2, B_W)), full((2, 2 * DECAY_LORA, B_W)),
                  full((2, B_W)), full((2, 2 * AAA_LORA, B_W)), full((2, B_W)), full((1, B_W)),
                  full((GATE_LORA, B_W)), full((B_W, B_W)),
                  full((2, tm, tm)), full((2, nct * 8, tm))],
        out_specs=(dir_tok, dir_tok, dir_tok, dir_tok, tok(B_W),
                   pl.BlockSpec((1, 2, 1, nct * 8, B_W), lambda bi, i: (bi, 0, i, 0, 0)),
                   tok(B_W), tok(B_W)),
        compiler_params=_cparams(("parallel", "parallel")),
        name="prep",
    )(kvr, kvr, kvr, lora, gd, p["k_k"], p["decay_w0"], p["wdec"], p["iclr_a0"], p["wicl"],
      p["k_a"], p["r_k"], p["wgate"], p["bd"], dmat, smat)
    at, bt, kt, rt, vo, vec, bv, gate = outs
    vec = vec.reshape(b, 2, t // chunk, 8, B_W)
    return at, bt, kt, rt, vo, vec, bv, gate


def _rwkv_kernel(at0, bt0, kt0, rt0, v0, vec0, at1, bt1, kt1, rt1, v1, vec1, s0_ref,
                 y0_ref, y1_ref, sfin_ref, z_ref, *, chunk, nb):
    i = pl.program_id(1)
    L = chunk
    assert L & (L - 1) == 0
    a_refs, b_refs, k_refs, r_refs = (at0, at1), (bt0, bt1), (kt0, kt1), (rt0, rt1)
    v_refs, vec_refs, y_refs = (v0, v1), (vec0, vec1), (y0_ref, y1_ref)

    @pl.when(i == 0)
    def _():
        z_ref[...] = s0_ref[...]

    rowi = lax.broadcasted_iota(jnp.int32, (L, L), 0)
    coli = lax.broadcasted_iota(jnp.int32, (L, L), 1)
    strict = (rowi > coli, rowi < coli)
    incl = (rowi >= coli, rowi <= coli)
    eye_f = (rowi == coli).astype(F32)
    blk_xor = rowi ^ coli
    lane = lax.broadcasted_iota(jnp.int32, (L, LANES), 1)
    head0 = lane < B_HEAD
    sub = lax.broadcasted_iota(jnp.int32, (LANES, LANES), 0)
    lan = lax.broadcasted_iota(jnp.int32, (LANES, LANES), 1)
    same_head = (sub < B_HEAD) == (lan < B_HEAD)
    eye_c = sub == lan

    def split_heads(x):
        zero = jnp.zeros_like(x)
        return jnp.where(head0, x, zero), jnp.where(head0, zero, x)

    def stack_heads(x):
        x0, x1 = split_heads(x)
        return jnp.concatenate([x0, x1], axis=0)

    sls = [slice(p * LANES, (p + 1) * LANES) for p in range(N_PAIRS)]
    dps = [(n, d, p) for n in range(nb) for d in (0, 1) for p in range(N_PAIRS)]
    a_t = {(n, d, p): a_refs[d][n, 0, :, sls[p]] for n, d, p in dps}
    b_t = {(n, d, p): b_refs[d][n, 0, :, sls[p]] for n, d, p in dps}
    k_t = {(n, d, p): k_refs[d][n, 0, :, sls[p]] for n, d, p in dps}
    r_t = {(n, d, p): r_refs[d][n, 0, :, sls[p]] for n, d, p in dps}
    vv = {(n, d, p): v_refs[d][n, :, sls[p]] for n, d, p in dps}
    em = {(n, d, p): vec_refs[d][n, 0, 0, 0:1, sls[p]] for n, d, p in dps}
    pl_row = {(n, d, p): vec_refs[d][n, 0, 0, 1:2, sls[p]] for n, d, p in dps}
    epl = {(n, d, p): vec_refs[d][n, 0, 0, 2:3, sls[p]] for n, d, p in dps}
    z0 = {dp: z_ref[dp] for dp in dps}
    z0b = {dp: z0[dp].astype(BF16) for dp in dps}
    a_h = {dp: split_heads(a_t[dp]) for dp in dps}
    r_h = {dp: split_heads(r_t[dp]) for dp in dps}

    heads = [dp + (h,) for dp in dps for h in range(2)]
    tt, mf, mak, arb, ark = {}, {}, {}, {}, {}
    for ph in heads:
        dp, d, h = ph[:3], ph[1], ph[3]
        bk = jnp.concatenate([b_t[dp], k_t[dp]], axis=0)
        res = _dot_nt(jnp.concatenate([a_h[dp][h], r_h[dp][h]], axis=0), bk)
        mab = jnp.where(strict[d], res[0:L, 0:L], 0.0)
        mak[ph] = jnp.where(strict[d], res[0:L, L:2 * L], 0.0).astype(BF16)
        arb[ph] = jnp.where(incl[d], res[L:2 * L, 0:L], 0.0).astype(BF16)
        ark[ph] = jnp.where(incl[d], res[L:2 * L, L:2 * L], 0.0).astype(BF16)
        tt[ph] = eye_f + jnp.where(blk_xor == 1, mab, 0.0)
        mf[ph] = mab
    def take(x, s, d):
        if s % SUBLANES:
            return x
        first = 1 - d
        return jnp.concatenate([x[(2 * j + first) * s:(2 * j + first + 1) * s] for j in range(L // (2 * s))],
                               axis=0)

    def put(x, s, d):
        if s % SUBLANES:
            return x
        zero = jnp.zeros((s, x.shape[1]), x.dtype)
        pieces = []
        for j in range(L // (2 * s)):
            blk = x[j * s:(j + 1) * s]
            pieces += [zero, blk] if d == 0 else [blk, zero]
        return jnp.concatenate(pieces, axis=0)

    s = 2
    while s < L:
        level = (blk_xor >= s) & (blk_xor < 2 * s)
        xx = {ph: _dot(take(mf[ph], s, ph[1]).astype(BF16), tt[ph].astype(BF16)) for ph in heads}
        for ph in heads:
            w = _dot(take(tt[ph], s, ph[1]).astype(BF16), put(xx[ph], s, ph[1]).astype(BF16))
            tt[ph] = tt[ph] + jnp.where(level, put(w, s, ph[1]), 0.0)
        s *= 2

    v_stack = {dp: stack_heads(vv[dp]) for dp in dps}
    mv = {dp: _dot(jnp.concatenate([mak[dp + (0,)], mak[dp + (1,)]], axis=1), v_stack[dp]) for dp in dps}
    wu = {}
    for dp in dps:
        x_rhs = jnp.concatenate([jnp.concatenate(list(a_h[dp]), axis=0),
                                 stack_heads(mv[dp].astype(BF16))], axis=1)
        t_cat = jnp.concatenate([tt[dp + (0,)].astype(BF16), tt[dp + (1,)].astype(BF16)], axis=1)
        wu[dp] = _dot(t_cat, x_rhs)
    u = {dp: _dot((wu[dp][:, 0:LANES] * em[dp]).astype(BF16), z0b[dp]) + wu[dp][:, LANES:2 * LANES]
         for dp in dps}
    ub = {dp: u[dp].astype(BF16) for dp in dps}
    for dp in dps:
        n, d, p = dp
        y = (_dot((r_t[dp].astype(F32) * em[dp]).astype(BF16), z0b[dp])
             + _dot(jnp.concatenate([arb[dp + (0,)], arb[dp + (1,)]], axis=1), stack_heads(ub[dp]))
             + _dot(jnp.concatenate([ark[dp + (0,)], ark[dp + (1,)]], axis=1), v_stack[dp]))
        y_refs[d][n, :, sls[p]] = y
    for dp in dps:
        bhat_t = jnp.transpose(b_t[dp].astype(F32) * epl[dp]).astype(BF16)
        khat_t = jnp.transpose(k_t[dp].astype(F32) * epl[dp]).astype(BF16)
        upd = _dot(jnp.concatenate([bhat_t, khat_t], axis=1), jnp.concatenate([ub[dp], vv[dp]], axis=0))
        pl_col = jnp.sum(jnp.where(eye_c, pl_row[dp], 0.0), axis=1, keepdims=True)
        z_ref[dp] = pl_col * z0[dp] + jnp.where(same_head, upd, 0.0)

    sfin_ref[...] = z_ref[...]


def _rwkv(at, bt, kt, rt, vo, vec, s0, chunk):
    b, _, t, _ = at.shape
    nc = t // chunk
    nb = RW_NB if b % RW_NB == 0 else 1
    fwd_tok = pl.BlockSpec((nb, 1, chunk, B_W), lambda bi, i: (bi, 0, i, 0))
    rev_tok = pl.BlockSpec((nb, 1, chunk, B_W), lambda bi, i: (bi, 1, nc - 1 - i, 0))
    fwd_v = pl.BlockSpec((nb, chunk, B_W), lambda bi, i: (bi, i, 0))
    rev_v = pl.BlockSpec((nb, chunk, B_W), lambda bi, i: (bi, nc - 1 - i, 0))
    fwd_vec = pl.BlockSpec((nb, 1, 1, 8, B_W), lambda bi, i: (bi, 0, i, 0, 0))
    rev_vec = pl.BlockSpec((nb, 1, 1, 8, B_W), lambda bi, i: (bi, 1, nc - 1 - i, 0, 0))
    state = pl.BlockSpec((nb, 2, N_PAIRS, LANES, LANES), lambda bi, i: (bi, 0, 0, 0, 0))
    kern = functools.partial(_rwkv_kernel, chunk=chunk, nb=nb)
    return pl.pallas_call(
        kern,
        out_shape=(jax.ShapeDtypeStruct((b, t, B_W), F32),
                   jax.ShapeDtypeStruct((b, t, B_W), F32),
                   jax.ShapeDtypeStruct((b, 2, N_PAIRS, LANES, LANES), F32)),
        grid=(b // nb, nc),
        in_specs=[fwd_tok, fwd_tok, fwd_tok, fwd_tok, fwd_v, fwd_vec,
                  rev_tok, rev_tok, rev_tok, rev_tok, rev_v, rev_vec, state],
        out_specs=(fwd_v, rev_v, state),
        scratch_shapes=[pltpu.VMEM((nb, 2, N_PAIRS, LANES, LANES), F32)],
        compiler_params=_cparams(("parallel", "arbitrary")),
        name="rwkv",
    )(at, bt, kt, rt, vo, vec, at, bt, kt, rt, vo, vec, s0)


def _post_kernel(y0_ref, y1_ref, bv_ref, gate_ref, u_ref, va_ref, x_ref, mod_ref,
                 lnxg_ref, lnxb_ref, gmg_ref, gmb_ref, wsp_ref, bsp_ref, bd_ref,
                 wout_ref, gpost_ref, gpre_ref, wr_ref, br_ref,
                 x1_ref, h2_ref, comb_ref, *, tm):
    bdm = bd_ref[...]
    inv = 1.0 / B_HEAD

    def group_norm(val, eps):
        mu = _split_dot(val, bdm) * inv
        cen = val - mu
        var = _dot((cen * cen).astype(BF16), bdm) * inv
        return cen * lax.rsqrt(var + eps)

    y = y0_ref[0] + y1_ref[0]
    yb = group_norm(y, GN_EPS) * lnxg_ref[...] + lnxb_ref[...]
    out_b = (yb + bv_ref[0]) * gate_ref[0]

    uh = jax.nn.gelu(u_ref[0].astype(F32))
    vg = jax.nn.gelu(va_ref[0].astype(F32))
    vn = (group_norm(vg, LN_EPS) * gmg_ref[...] + gmb_ref[...]).astype(BF16)
    lane = lax.broadcasted_iota(jnp.int32, (GM_CHUNK, LANES), 1)
    head0 = lane < B_HEAD
    s_rows = []
    for c in range(tm // GM_CHUNK):
        cols = []
        for p in range(A_W // LANES):
            blk = vn[c * GM_CHUNK:(c + 1) * GM_CHUNK, p * LANES:(p + 1) * LANES]
            zero = jnp.zeros_like(blk)
            stack = jnp.concatenate([jnp.where(head0, blk, zero), jnp.where(head0, zero, blk)], axis=0)
            cols.append(_dot(wsp_ref[p], stack))
        s_rows.append(jnp.concatenate(cols, axis=1) + bsp_ref[...])
    s = jnp.concatenate(s_rows, axis=0)
    out_a = uh * s

    ymix = (_dot(out_a.astype(BF16), wout_ref[0:A_W, :]) + _dot(out_b.astype(BF16), wout_ref[A_W:, :]))
    gate1 = mod_ref[0, 0:1, :]
    shift2 = mod_ref[0, 1:2, :]
    scale2 = mod_ref[0, 2:3, :]
    ms = jnp.mean(ymix * ymix, axis=-1, keepdims=True)
    x1 = x_ref[0] + gate1 * (ymix * lax.rsqrt(ms + NORM_EPS) * gpost_ref[...])
    x1_ref[0] = x1
    ms2 = jnp.mean(x1 * x1, axis=-1, keepdims=True)
    h2 = x1 * lax.rsqrt(ms2 + NORM_EPS) * gpre_ref[...]
    h2 = h2 * (1.0 + scale2) + shift2
    for c in range(ROW_CHUNKS):
        h2_ref[0, :, c, :] = h2[:, c * LANES:(c + 1) * LANES]

    h2_hi = h2.astype(BF16)
    h2_lo = (h2 - h2_hi.astype(F32)).astype(BF16)
    logits = (_dot(h2_hi, wr_ref[0]) + _dot(h2_lo, wr_ref[0]) + _dot(h2_hi, wr_ref[1])) + br_ref[...]
    lni = lax.broadcasted_iota(jnp.int32, logits.shape, 1)
    ln = lni.astype(F32)
    lgrp = (lni // EXPERTS_PER_GROUP).astype(F32)
    neg = jnp.float32(-jnp.inf)
    big = jnp.float32(LANES)
    gmask = (lni >= N_EXPERTS) & (lni < N_EXPERTS + N_GROUPS)
    gl = jnp.where(gmask, logits, neg)
    gmax = jnp.max(gl, axis=-1, keepdims=True)
    gsum = jnp.sum(jnp.where(gmask, jnp.exp(gl - gmax), 0.0), axis=-1, keepdims=True)
    g_w = 1.0 / gsum
    g_i = jnp.min(jnp.where(gl == gmax, ln - N_EXPERTS, big), axis=-1, keepdims=True)
    emask = (lni < N_EXPERTS) & (lgrp == g_i)
    el = jnp.where(emask, logits, neg)
    l1 = jnp.max(el, axis=-1, keepdims=True)
    i1 = jnp.min(jnp.where(el == l1, ln, big), axis=-1, keepdims=True)
    el2 = jnp.where(ln == i1, neg, el)
    l2 = jnp.max(el2, axis=-1, keepdims=True)
    i2 = jnp.min(jnp.where(el2 == l2, ln, big), axis=-1, keepdims=True)
    e21 = jnp.exp(l2 - l1)
    w1 = 1.0 / (1.0 + e21)
    w2 = e21 * w1
    comb_ref[0] = (g_w * (jnp.where(ln == i1, w1, 0.0) + jnp.where(ln == i2, w2, 0.0))
                   + jnp.where((ln == i1 + MEMB) | (ln == i2 + MEMB), 1.0, 0.0))


def _post(y0, y1, bv, gate, u, va, x, mod3, p, tm):
    b, t, _ = x.shape
    tok = lambda width: pl.BlockSpec((1, tm, width), lambda bi, i: (bi, i, 0))
    full = lambda shape: pl.BlockSpec(shape, lambda bi, i: (0,) * len(shape))
    kern = functools.partial(_post_kernel, tm=tm)
    return pl.pallas_call(
        kern,
        out_shape=(jax.ShapeDtypeStruct((b, t, D_MODEL), F32),
                   jax.ShapeDtypeStruct((b, t, ROW_CHUNKS, LANES), F32),
                   jax.ShapeDtypeStruct((b, t, LANES), F32)),
        grid=(b, t // tm),
        in_specs=[tok(B_W), tok(B_W), tok(B_W), tok(B_W), tok(A_W), tok(A_W), tok(D_MODEL),
                  pl.BlockSpec((1, 8, D_MODEL), lambda bi, i: (bi, 0, 0)),
                  full((1, B_W)), full((1, B_W)), full((1, A_W)), full((1, A_W)),
                  full((A_W // LANES, GM_CHUNK, 2 * GM_CHUNK)), full((GM_CHUNK, A_W)), full((B_W, B_W)),
                  full((D_MODEL, D_MODEL)), full((1, D_MODEL)), full((1, D_MODEL)),
                  full((2, D_MODEL, LANES)), full((1, LANES))],
        out_specs=(tok(D_MODEL), pl.BlockSpec((1, tm, ROW_CHUNKS, LANES), lambda bi, i: (bi, i, 0, 0)),
                   tok(LANES)),
        compiler_params=_cparams(("parallel", "parallel")),
        name="post",
    )(y0, y1, bv, gate, u, va, x, mod3, p["lnx_g"], p["lnx_b"], p["gm_ln_g"], p["gm_ln_b"],
      p["wsp"], p["bsp"], p["bd"], p["w_out"], p["g_post1"], p["g_pre2"], p["w_router"], p["b_router"])


def _split3(x):
    hi = x.astype(BF16)
    r1 = x - hi.astype(F32)
    mid = r1.astype(BF16)
    lo = (r1 - mid.astype(F32)).astype(BF16)
    return hi, mid, lo


def _route_kernel(comb_ref, tri_ref, upper_ref, dst_ref, meta_ref, cnt_ref, carry_ref, offs_ref, ends_ref,
                  *, tm, tr, ntp):
    ph = pl.program_id(0)
    i = pl.program_id(1)
    last = pl.num_programs(1) - 1
    comb = comb_ref[...]
    lni = lax.broadcasted_iota(jnp.int32, (tm, LANES), 1)
    member = jnp.where((lni >= MEMB) & (lni < MEMB + N_EXPERTS), comb, 0.0)
    colsum = jnp.sum(member, axis=0, keepdims=True)

    @pl.when((ph == 0) & (i == 0))
    def _():
        cnt_ref[...] = jnp.zeros_like(cnt_ref)

    @pl.when(ph == 0)
    def _():
        cnt_ref[...] += colsum

    @pl.when((ph == 1) & (i == 0))
    def _():
        cnt = cnt_ref[...]
        padded = jnp.floor((cnt + (tr - 1)) * (1.0 / tr)) * tr
        hi, mid, lo = _split3(padded)
        up = upper_ref[...]
        ends = _dot(hi, up) + _dot(mid, up) + _dot(lo, up)
        ends_ref[...] = ends
        offs_ref[...] = ends - padded
        carry_ref[...] = jnp.zeros_like(carry_ref)

    @pl.when(ph == 1)
    def _():
        cum = _dot(tri_ref[...], member.astype(BF16))
        pos = offs_ref[0:1, :] + carry_ref[0:1, :] + cum
        lnf = lni.astype(F32)
        ea = jnp.min(jnp.where(member > 0, lnf, 2.0 * LANES), axis=-1, keepdims=True)
        eb = jnp.max(jnp.where(member > 0, lnf, -1.0), axis=-1, keepdims=True)
        ones8 = jnp.ones((8, LANES), BF16)
        for slot, e_sel in enumerate((ea, eb)):
            hi, mid, lo = _split3(jnp.where(lnf == e_sel, pos, 0.0))
            row = _dot_nt(ones8, hi) + _dot_nt(ones8, mid) + _dot_nt(ones8, lo)
            dst_ref[0, :, slot * tm:(slot + 1) * tm] = row.astype(jnp.int32)
        carry_ref[...] += colsum

    @pl.when((ph == 1) & (i == last))
    def _():
        ends = ends_ref[...]
        r = lax.broadcasted_iota(jnp.int32, (LANES, LANES), 0)
        c = lax.broadcasted_iota(jnp.int32, (LANES, LANES), 1)
        ends_col = jnp.sum(jnp.where(r == c, ends[0:1, :], 0.0), axis=1, keepdims=True)
        sub = lax.broadcasted_iota(jnp.int32, (LANES, ntp), 0)
        start = lax.broadcasted_iota(jnp.int32, (LANES, ntp), 1).astype(F32) * float(tr)
        is_exp = (sub >= MEMB) & (sub < MEMB + N_EXPERTS)
        done = jnp.where(is_exp & (ends_col <= start), 1.0, 0.0)
        tile_expert = jnp.minimum(jnp.sum(done, axis=0, keepdims=True), N_EXPERTS - 1.0)
        n_used = jnp.max(ends[0:1, :], axis=-1, keepdims=True) * (1.0 / tr)
        rowi = lax.broadcasted_iota(jnp.int32, (8, ntp), 0)
        meta_ref[...] = jnp.where(rowi == 0, tile_expert, n_used).astype(jnp.int32)


def _route(comb2d, tm, tr, ntp):
    n = comb2d.shape[0]
    nt = n // tm
    t = np.arange(tm)
    tri = jnp.asarray(t[None, :] < t[:, None], BF16)
    e = np.arange(LANES)
    upper = jnp.asarray(e[:, None] <= e[None, :], BF16)
    kern = functools.partial(_route_kernel, tm=tm, tr=tr, ntp=ntp)
    return pl.pallas_call(
        kern,
        out_shape=(jax.ShapeDtypeStruct((nt, 8, 2 * tm), jnp.int32),
                   jax.ShapeDtypeStruct((8, ntp), jnp.int32)),
        grid=(2, nt),
        in_specs=[pl.BlockSpec((tm, LANES), lambda ph, i: (i, 0)),
                  pl.BlockSpec((tm, tm), lambda ph, i: (0, 0)),
                  pl.BlockSpec((LANES, LANES), lambda ph, i: (0, 0))],
        out_specs=(pl.BlockSpec((1, 8, 2 * tm), lambda ph, i: (i * ph, 0, 0)),
                   pl.BlockSpec((8, ntp), lambda ph, i: (0, 0))),
        scratch_shapes=[pltpu.VMEM((8, LANES), F32)] * 4,
        compiler_params=_cparams(("arbitrary", "arbitrary")),
        name="route",
    )(comb2d, tri, upper)


DMA_UNROLL = 8


def _dispatch_kernel(dst_ref, h_ref, xs_in, xs_hbm, sem, *, tm):
    del xs_in

    def issue(blk, carry):
        for k in range(DMA_UNROLL):
            t = blk * DMA_UNROLL + k
            pltpu.make_async_copy(h_ref.at[t], xs_hbm.at[dst_ref[0, 0, t]], sem.at[0]).start()
            pltpu.make_async_copy(h_ref.at[t], xs_hbm.at[dst_ref[0, 0, tm + t]], sem.at[1]).start()
        return carry

    lax.fori_loop(0, tm // DMA_UNROLL, issue, 0)
    pltpu.make_async_copy(h_ref, xs_hbm.at[pl.ds(0, tm)], sem.at[0]).wait()
    pltpu.make_async_copy(h_ref, xs_hbm.at[pl.ds(0, tm)], sem.at[1]).wait()


def _dispatch(dst, h3, n_rows, tm):
    n = h3.shape[0]
    xs0 = jnp.zeros((n_rows, ROW_CHUNKS, LANES), F32)
    kern = functools.partial(_dispatch_kernel, tm=tm)
    return pl.pallas_call(
        kern,
        out_shape=jax.ShapeDtypeStruct((n_rows, ROW_CHUNKS, LANES), F32),
        grid=(n // tm,),
        in_specs=[pl.BlockSpec((1, 1, 2 * tm), lambda i: (i, 0, 0), memory_space=pltpu.SMEM),
                  pl.BlockSpec((tm, ROW_CHUNKS, LANES), lambda i: (i, 0, 0)),
                  pl.BlockSpec(memory_space=pl.ANY)],
        out_specs=pl.BlockSpec(memory_space=pl.ANY),
        scratch_shapes=[pltpu.SemaphoreType.DMA((2,))],
        input_output_aliases={2: 0},
        compiler_params=_cparams(("arbitrary",)),
        name="dispatch",
    )(dst, h3, xs0)


def _gmm_kernel(meta_ref, xs_ref, wg_ref, wu_ref, wd_ref, ys_ref):
    j = pl.program_id(0)
    n_used = meta_ref[1, 0]

    @pl.when(j < n_used)
    def _():
        x = jnp.concatenate([xs_ref[:, c, :] for c in range(ROW_CHUNKS)], axis=1).astype(BF16)
        gate = _dot(x, wg_ref[0])
        up = _dot(x, wu_ref[0])
        hid = (gate * jax.nn.sigmoid(gate) * up).astype(BF16)
        y = _dot(hid, wd_ref[0])
        for c in range(ROW_CHUNKS):
            ys_ref[:, c, :] = y[:, c * LANES:(c + 1) * LANES]

    @pl.when(j >= n_used)
    def _():
        ys_ref[...] = jnp.zeros_like(ys_ref)


def _gmm(meta, xs, wg, wu, wd, tr):
    n_rows = xs.shape[0]
    row_spec = lambda fn: pl.BlockSpec((tr, ROW_CHUNKS, LANES), fn)
    return pl.pallas_call(
        _gmm_kernel,
        out_shape=jax.ShapeDtypeStruct(xs.shape, F32),
        grid_spec=pltpu.PrefetchScalarGridSpec(
            num_scalar_prefetch=1,
            grid=(n_rows // tr,),
            in_specs=[row_spec(lambda j, m: (jnp.minimum(j, m[1, 0] - 1), 0, 0)),
                      pl.BlockSpec((1, D_MODEL, D_EXPERT), lambda j, m: (m[0, j], 0, 0)),
                      pl.BlockSpec((1, D_MODEL, D_EXPERT), lambda j, m: (m[0, j], 0, 0)),
                      pl.BlockSpec((1, D_EXPERT, D_MODEL), lambda j, m: (m[0, j], 0, 0))],
            out_specs=row_spec(lambda j, m: (j, 0, 0))),
        compiler_params=_cparams(("arbitrary",)),
        name="gmm",
    )(meta, xs, wg, wu, wd)


def _combine_kernel(dst_ref, ys_hbm, comb_ref, x1_ref, mod_ref, gpost_ref, o_ref, buf_a, buf_b, sem, *, tm):
    def issue(blk, carry):
        for k in range(DMA_UNROLL):
            t = blk * DMA_UNROLL + k
            pltpu.make_async_copy(ys_hbm.at[dst_ref[0, 0, t]], buf_a.at[t], sem.at[0]).start()
            pltpu.make_async_copy(ys_hbm.at[dst_ref[0, 0, tm + t]], buf_b.at[t], sem.at[1]).start()
        return carry

    lax.fori_loop(0, tm // DMA_UNROLL, issue, 0)
    pltpu.make_async_copy(ys_hbm.at[pl.ds(0, tm)], buf_a, sem.at[0]).wait()
    pltpu.make_async_copy(ys_hbm.at[pl.ds(0, tm)], buf_b, sem.at[1]).wait()

    comb = comb_ref[0]
    lni = lax.broadcasted_iota(jnp.int32, comb.shape, 1)
    lnf = lni.astype(F32)
    member = (lni >= MEMB) & (lni < MEMB + N_EXPERTS) & (comb > 0)
    ea = jnp.min(jnp.where(member, lnf, 2.0 * LANES), axis=-1, keepdims=True) - MEMB
    eb = jnp.max(jnp.where(member, lnf, -1.0), axis=-1, keepdims=True) - MEMB
    wa = jnp.sum(jnp.where(lnf == ea, comb, 0.0), axis=-1, keepdims=True)
    wb = jnp.sum(jnp.where(lnf == eb, comb, 0.0), axis=-1, keepdims=True)
    ya = jnp.concatenate([buf_a[:, c, :] for c in range(ROW_CHUNKS)], axis=1)
    yb = jnp.concatenate([buf_b[:, c, :] for c in range(ROW_CHUNKS)], axis=1)
    y = wa * ya + wb * yb
    ms = jnp.mean(y * y, axis=-1, keepdims=True)
    o_ref[0] = x1_ref[0] + mod_ref[0, 3:4, :] * (y * lax.rsqrt(ms + NORM_EPS) * gpost_ref[...])


def _combine(dst, ys, comb, x1, mod3, g_post2, tm):
    b, t, _ = x1.shape
    nti = t // tm
    tok = lambda width: pl.BlockSpec((1, tm, width), lambda bi, i: (bi, i, 0))
    kern = functools.partial(_combine_kernel, tm=tm)
    return pl.pallas_call(
        kern,
        out_shape=jax.ShapeDtypeStruct((b, t, D_MODEL), F32),
        grid=(b, nti),
        in_specs=[pl.BlockSpec((1, 1, 2 * tm), lambda bi, i: (bi * nti + i, 0, 0), memory_space=pltpu.SMEM),
                  pl.BlockSpec(memory_space=pl.ANY),
                  tok(LANES), tok(D_MODEL),
                  pl.BlockSpec((1, 8, D_MODEL), lambda bi, i: (bi, 0, 0)),
                  pl.BlockSpec((1, D_MODEL), lambda bi, i: (0, 0))],
        out_specs=tok(D_MODEL),
        scratch_shapes=[pltpu.VMEM((tm, ROW_CHUNKS, LANES), F32), pltpu.VMEM((tm, ROW_CHUNKS, LANES), F32),
                        pltpu.SemaphoreType.DMA((2,))],
        compiler_params=_cparams(("arbitrary", "arbitrary")),
        name="combine",
    )(dst, ys, comb, x1, mod3, g_post2.reshape(1, D_MODEL))


def _moe(h3, comb, x1, mod3, wg, wu, wd, g_post2):
    b, t, _ = x1.shape
    n = b * t
    tm = _tile(t, MOE_TM)
    n_rows = 2 * n + N_EXPERTS * MOE_TR
    ntp = -(-(n_rows // MOE_TR) // LANES) * LANES
    dst8, meta = _route(comb.reshape(n, LANES), tm, MOE_TR, ntp)
    dst = dst8[:, 0:1, :]
    xs = _dispatch(dst, h3.reshape(n, ROW_CHUNKS, LANES), n_rows, tm)
    ys = _gmm(meta[0:2], xs, wg, wu, wd, MOE_TR)
    return _combine(dst, ys, comb, x1, mod3, g_post2, tm)


def _tile(t, pref):
    while t % pref:
        pref //= 2
    return pref


def kernel(x, c, ctx, c_ctx, w_mod, b_mod, g_pre1, g_post1, g_pre2, g_post2, w_in, conv_rkv, gm_ln_g, gm_ln_b, w_spatial, b_spatial, decay_w0, decay_up, iclr_a0, iclr_up, k_k, k_a, r_k, outgate_up, lnx_g, lnx_b, w_out, w_router_grp, b_router_grp, w_router_exp, b_router_exp, w_gate, w_up, w_down):
    assert w_mod.shape[0] == 1, "single-layer kernel"
    b, t, _ = x.shape
    tc = ctx.shape[1]
    assert b <= 7 and t % RW_CHUNK == 0 and tc % RW_CHUNK == 0 and t % GRID_W == 0

    head_of = np.arange(B_W) // B_HEAD
    bd = jnp.asarray(head_of[:, None] == head_of[None, :], BF16)
    zpad = jnp.zeros((2, DECAY_LORA, B_W), F32)
    wdec = jnp.stack([jnp.concatenate([decay_up[0, 0], zpad[0]], 0),
                      jnp.concatenate([zpad[0], decay_up[0, 1]], 0)]).astype(BF16)
    wicl = jnp.stack([jnp.concatenate([iclr_up[0, 0], zpad[0]], 0),
                      jnp.concatenate([zpad[0], iclr_up[0, 1]], 0)]).astype(BF16)
    wsp = w_spatial[0].reshape(A_W // LANES, 2, GM_CHUNK, GM_CHUNK)
    wsp = jnp.concatenate([wsp[:, 0], wsp[:, 1]], axis=-1).astype(BF16)
    bsp = jnp.repeat(b_spatial[0].T, A_W // A_GROUPS, axis=1)
    w_router = jnp.concatenate(
        [w_router_exp[0], w_router_grp[0],
         jnp.zeros((D_MODEL, LANES - N_EXPERTS - N_GROUPS), F32)], axis=1)
    w_router_hi = w_router.astype(BF16)
    w_router = jnp.stack([w_router_hi, (w_router - w_router_hi.astype(F32)).astype(BF16)])
    b_router = jnp.concatenate(
        [b_router_exp[0], b_router_grp[0], jnp.zeros((LANES - N_EXPERTS - N_GROUPS,), F32)]).reshape(1, LANES)
    prm = {
        "k_k": k_k[0].reshape(1, B_W), "decay_w0": decay_w0[0], "wdec": wdec,
        "iclr_a0": iclr_a0[0], "wicl": wicl, "k_a": k_a[0], "r_k": r_k[0].reshape(1, B_W),
        "wgate": outgate_up[0].astype(BF16), "bd": bd,
        "lnx_g": lnx_g[0].reshape(1, B_W), "lnx_b": lnx_b[0].reshape(1, B_W),
        "gm_ln_g": gm_ln_g[0].reshape(1, A_W), "gm_ln_b": gm_ln_b[0].reshape(1, A_W),
        "wsp": wsp, "bsp": bsp, "w_out": w_out[0].astype(BF16),
        "g_post1": g_post1[0].reshape(1, D_MODEL), "g_pre2": g_pre2[0].reshape(1, D_MODEL),
        "w_router": w_router, "b_router": b_router,
    }
    w_in_b = w_in[0].astype(BF16)

    cond8 = jnp.concatenate([c, c_ctx[None, :], jnp.zeros((8 - b - 1, D_MODEL), F32)], axis=0)
    mod = _mod(cond8, w_mod[0], b_mod[0])
    mod6 = mod.reshape(8, 6, D_MODEL)
    shift1, scale1 = mod6[:b, 0:1], mod6[:b, 1:2]
    cshift1 = jnp.broadcast_to(mod6[b:b + 1, 0:1], (b, 1, D_MODEL))
    cscale1 = jnp.broadcast_to(mod6[b:b + 1, 1:2], (b, 1, D_MODEL))
    mod3 = jnp.concatenate([mod6[:b, 2:6], jnp.zeros((b, 4, D_MODEL), F32)], axis=1)

    c_rkv, c_lora, c_gd, _, _ = _inproj(ctx, g_pre1[0], cshift1, cscale1, w_in_b, _tile(tc, 256))
    c_kvr = _conv(c_rkv, conv_rkv[0], tc, _tile(tc, 256), 3 * B_W)
    c_at, c_bt, c_kt, c_rt, c_v, c_vec, _, _ = _prep(c_kvr, c_lora, c_gd, prm, _tile(tc, 256), RW_CHUNK)
    s_zero = jnp.zeros((b, 2, N_PAIRS, LANES, LANES), F32)
    _, _, states = _rwkv(c_at, c_bt, c_kt, c_rt, c_v, c_vec, s_zero, RW_CHUNK)

    rkv, lora, gd, u, va = _inproj(x, g_pre1[0], shift1, scale1, w_in_b, _tile(t, 512))
    kvr = _conv(rkv, conv_rkv[0], GRID_W, _tile(t, 512), 3 * B_W)
    at, bt, kt, rt, vo, vec, bv, gate = _prep(kvr, lora, gd, prm, _tile(t, 256), RW_CHUNK)
    y0, y1, _ = _rwkv(at, bt, kt, rt, vo, vec, states, RW_CHUNK)
    x1, h2, comb = _post(y0, y1, bv, gate, u, va, x, mod3, prm, _tile(t, 256))

    return _moe(h2, comb, x1, mod3, w_gate[0].astype(BF16), w_up[0].astype(BF16), w_down[0].astype(BF16),
                g_post2[0])
```

```python
import functools
import math

import numpy as np
import jax
import jax.numpy as jnp
from jax import lax
from jax.experimental import pallas as pl
from jax.experimental.pallas import tpu as pltpu

D_MODEL = 1024
GRID_W = 64
A_W = 512
A_GROUPS = 8
GM_CHUNK = 128
B_W = 512
B_HEADS = 8
B_HEAD = 64
DECAY_LORA = 64
AAA_LORA = 64
GATE_LORA = 128
OFF_WD = 3 * B_W
OFF_GD = OFF_WD + 2 * DECAY_LORA + 2 * AAA_LORA
OFF_U = OFF_GD + GATE_LORA
OFF_VA = OFF_U + A_W
D_IN = OFF_VA + A_W
N_GROUPS = 4
EXPERTS_PER_GROUP = 8
N_EXPERTS = 32
D_EXPERT = 256
NORM_EPS = 1e-6
LN_EPS = 1e-5
GN_EPS = 64e-5

LANES = 128
SUBLANES = 8
RW_CHUNK = 128
RW_NB = 2
N_PAIRS = B_W // LANES
ROW_CHUNKS = D_MODEL // LANES
MOE_TM = 512
MOE_TR = 512
MEMB = 64
VMEM_LIMIT = 48 * 1024 * 1024

BF16 = jnp.bfloat16
F32 = jnp.float32


def _dot(a, b):
    return jnp.dot(a, b, preferred_element_type=F32)


def _dot_nt(a, b):
    return lax.dot_general(a, b, (((1,), (1,)), ((), ())), preferred_element_type=F32)


def _split_dot(x, w_bf16):
    hi = x.astype(BF16)
    lo = (x - hi.astype(F32)).astype(BF16)
    return _dot(hi, w_bf16) + _dot(lo, w_bf16)


def _cparams(sem):
    return pltpu.CompilerParams(dimension_semantics=sem, vmem_limit_bytes=VMEM_LIMIT)


def _mod_kernel(c_ref, w_ref, b_ref, o_ref):
    c = c_ref[...]
    s = c * jax.nn.sigmoid(c)
    o_ref[...] = jnp.dot(s, w_ref[...], preferred_element_type=F32,
                         precision=lax.Precision.HIGHEST) + b_ref[...]


def _mod(cond8, w_mod, b_mod):
    n = w_mod.shape[1]
    tn = 1024
    return pl.pallas_call(
        _mod_kernel,
        out_shape=jax.ShapeDtypeStruct((8, n), F32),
        grid=(n // tn,),
        in_specs=[pl.BlockSpec((8, D_MODEL), lambda j: (0, 0)),
                  pl.BlockSpec((D_MODEL, tn), lambda j: (0, j)),
                  pl.BlockSpec((1, tn), lambda j: (0, j))],
        out_specs=pl.BlockSpec((8, tn), lambda j: (0, j)),
        compiler_params=_cparams(("arbitrary",)),
        name="mod",
    )(cond8, w_mod, b_mod.reshape(1, n))


def _inproj_kernel(x_ref, g_ref, sh_ref, sc_ref, w_ref, rkv_ref, lora_ref, gd_ref, u_ref, va_ref):
    x = x_ref[0]
    ms = jnp.mean(x * x, axis=-1, keepdims=True)
    h = x * lax.rsqrt(ms + NORM_EPS) * g_ref[...]
    h = h * (1.0 + sc_ref[0]) + sh_ref[0]
    hb = h.astype(BF16)
    rkv_ref[0] = _dot(hb, w_ref[:, 0:OFF_WD]).astype(rkv_ref.dtype)
    lora_ref[0] = _dot(hb, w_ref[:, OFF_WD:OFF_GD])
    gd_ref[0] = _dot(hb, w_ref[:, OFF_GD:OFF_U])
    u_ref[0] = _dot(hb, w_ref[:, OFF_U:OFF_VA]).astype(u_ref.dtype)
    va_ref[0] = _dot(hb, w_ref[:, OFF_VA:D_IN]).astype(va_ref.dtype)


def _inproj(x, g, shift, scale, w_in_bf16, tm):
    b, t, _ = x.shape
    tok = lambda width: pl.BlockSpec((1, tm, width), lambda bi, i: (bi, i, 0))
    vec = pl.BlockSpec((1, 1, D_MODEL), lambda bi, i: (bi, 0, 0))
    return pl.pallas_call(
        _inproj_kernel,
        out_shape=(jax.ShapeDtypeStruct((b, t, OFF_WD), BF16),
                   jax.ShapeDtypeStruct((b, t, OFF_GD - OFF_WD), F32),
                   jax.ShapeDtypeStruct((b, t, GATE_LORA), F32),
                   jax.ShapeDtypeStruct((b, t, A_W), BF16),
                   jax.ShapeDtypeStruct((b, t, A_W), BF16)),
        grid=(b, t // tm),
        in_specs=[tok(D_MODEL),
                  pl.BlockSpec((1, D_MODEL), lambda bi, i: (0, 0)),
                  vec, vec,
                  pl.BlockSpec((D_MODEL, D_IN), lambda bi, i: (0, 0))],
        out_specs=(tok(OFF_WD), tok(OFF_GD - OFF_WD), tok(GATE_LORA), tok(A_W), tok(A_W)),
        compiler_params=_cparams(("parallel", "parallel")),
        name="inproj",
    )(x, g.reshape(1, D_MODEL), shift, scale, w_in_bf16)


CONV_HALO = 128


def _conv_kernel(prev_ref, cur_ref, next_ref, w_ref, o_ref, e_ref, *, width, rows, tm):
    i = pl.program_id(1)
    last = pl.num_programs(1) - 1
    cb = o_ref.shape[-1]
    win = tm + 2 * CONV_HALO
    window = jnp.concatenate([
        jnp.where(i > 0, prev_ref[0].astype(F32), 0.0),
        cur_ref[0].astype(F32),
        jnp.where(i < last, next_ref[0].astype(F32), 0.0)], axis=0)
    spos = i * tm - CONV_HALO + lax.broadcasted_iota(jnp.int32, (win, cb), 0)
    col = spos & (width - 1)
    e_ref[0] = pltpu.roll(jnp.where(col == width - 1, 0.0, window), 1, 0)
    e_ref[1] = window
    e_ref[2] = pltpu.roll(jnp.where(col == 0, 0.0, window), win - 1, 0)
    acc = None
    for dr in (-1, 0, 1):
        if rows == 1 and dr != 0:
            continue
        for dc in (-1, 0, 1):
            xs = e_ref[dc + 1, pl.ds(CONV_HALO + dr * width, tm), :]
            term = xs * w_ref[dr + 1, dc + 1:dc + 2, :]
            acc = term if acc is None else acc + term
    o_ref[0] = acc


def _conv(z, w, width, tm, nch):
    b, t, _ = z.shape
    rows = t // width
    assert width & (width - 1) == 0 and (width + 1 <= CONV_HALO or rows == 1)
    cb = 512
    hb = tm // CONV_HALO
    nhalo = t // CONV_HALO
    kern = functools.partial(_conv_kernel, width=width, rows=rows, tm=tm)
    return pl.pallas_call(
        kern,
        out_shape=jax.ShapeDtypeStruct((b, t, nch), F32),
        grid=(b, t // tm, nch // cb),
        in_specs=[pl.BlockSpec((1, CONV_HALO, cb), lambda bi, i, c: (bi, jnp.maximum(i * hb - 1, 0), c)),
                  pl.BlockSpec((1, tm, cb), lambda bi, i, c: (bi, i, c)),
                  pl.BlockSpec((1, CONV_HALO, cb), lambda bi, i, c: (bi, jnp.minimum((i + 1) * hb, nhalo - 1), c)),
                  pl.BlockSpec((3, 3, cb), lambda bi, i, c: (0, 0, c))],
        out_specs=pl.BlockSpec((1, tm, cb), lambda bi, i, c: (bi, i, c)),
        scratch_shapes=[pltpu.VMEM((3, tm + 2 * CONV_HALO, cb), F32)],
        compiler_params=_cparams(("parallel", "parallel", "parallel")),
        name="conv",
    )(z, z, z, w)


def _prep_kernel(k_ref, v_ref, r_ref, lora_ref, gd_ref, kk_w_ref, w0_ref, wdec_ref, a0_ref, wicl_ref,
                 ka_ref, rk_ref, wgate_ref, bd_ref, dmat_ref, smat_ref,
                 at_ref, bt_ref, kt_ref, rt_ref, vo_ref, vec_ref, bv_ref, gate_ref):
    k = k_ref[0]
    v = v_ref[0]
    r = r_ref[0]
    lora = lora_ref[0]
    bd = bd_ref[...]
    kk = k * kk_w_ref[...]
    ss = _split_dot(kk * kk, bd)
    kk = kk * lax.rsqrt(jnp.maximum(ss, 1e-24))
    tl = jnp.tanh(lora[:, 0:2 * DECAY_LORA]).astype(BF16)
    la = lora[:, 2 * DECAY_LORA:].astype(BF16)
    kmod_sum = jnp.zeros_like(k)
    for d in range(2):
        logw = -math.exp(-0.5) * jax.nn.sigmoid(w0_ref[d:d + 1, :] + _dot(tl, wdec_ref[d]))
        a = jax.nn.sigmoid(a0_ref[d:d + 1, :] + _dot(la, wicl_ref[d]))
        kmod = k * (1.0 + (a - 1.0) * ka_ref[d:d + 1, :])
        kmod_sum = kmod_sum + kmod
        bvec = -(kk * a)
        rel = _split_dot_lhs(dmat_ref[d], logw)
        e_pos = jnp.exp(rel)
        e_neg = jnp.exp(-rel)
        at_ref[0, d] = (kk * jnp.exp(rel - logw)).astype(BF16)
        bt_ref[0, d] = (bvec * e_neg).astype(BF16)
        kt_ref[0, d] = (kmod * e_neg).astype(BF16)
        rt_ref[0, d] = (r * e_pos).astype(BF16)
        vec_ref[0, d, 0] = jnp.exp(_split_dot_lhs(smat_ref[d], logw))
    vo_ref[0] = v.astype(BF16)
    bonus = _split_dot(r * kmod_sum * rk_ref[...], bd)
    bv_ref[0] = bonus * v
    gate_ref[0] = _dot(jax.nn.sigmoid(gd_ref[0]).astype(BF16), wgate_ref[...])


def _split_dot_lhs(m_bf16, x):
    hi = x.astype(BF16)
    lo = (x - hi.astype(F32)).astype(BF16)
    return _dot(m_bf16, hi) + _dot(m_bf16, lo)


def _chunk_matrices(tm, chunk):
    t = np.arange(tm)
    same = (t[:, None] // chunk) == (t[None, :] // chunk)
    start = (t // chunk) * chunk
    half = chunk // 2
    d_fwd = same * ((t[None, :] <= t[:, None]).astype(np.float32)
                    - (t[None, :] < (start + half)[:, None]).astype(np.float32))
    d_rev = same * ((t[None, :] >= t[:, None]).astype(np.float32)
                    - (t[None, :] >= (start + half)[:, None]).astype(np.float32))
    nct = tm // chunk
    s_fwd = np.zeros((nct * 8, tm), np.float32)
    s_rev = np.zeros((nct * 8, tm), np.float32)
    for j in range(nct):
        in_chunk = (t // chunk) == j
        first = in_chunk & (t < j * chunk + half)
        second = in_chunk & (t >= j * chunk + half)
        s_fwd[8 * j + 0] = first
        s_fwd[8 * j + 1] = in_chunk
        s_fwd[8 * j + 2] = second
        s_rev[8 * j + 0] = second
        s_rev[8 * j + 1] = in_chunk
        s_rev[8 * j + 2] = first
    dmat = np.stack([d_fwd, d_rev]).astype(np.float32)
    smat = np.stack([s_fwd, s_rev])
    return jnp.asarray(dmat, BF16), jnp.asarray(smat, BF16)


def _prep(kvr, lora, gd, p, tm, chunk):
    b, t, _ = kvr.shape
    nct = tm // chunk
    dmat, smat = _chunk_matrices(tm, chunk)
    tokc = lambda c: pl.BlockSpec((1, tm, B_W), lambda bi, i: (bi, i, c))
    tok = lambda width: pl.BlockSpec((1, tm, width), lambda bi, i: (bi, i, 0))
    full = lambda shape: pl.BlockSpec(shape, lambda bi, i: (0,) * len(shape))
    dir_tok = pl.BlockSpec((1, 2, tm, B_W), lambda bi, i: (bi, 0, i, 0))
    dir_shape = jax.ShapeDtypeStruct((b, 2, t, B_W), BF16)
    outs = pl.pallas_call(
        _prep_kernel,
        out_shape=(dir_shape, dir_shape, dir_shape, dir_shape,
                   jax.ShapeDtypeStruct((b, t, B_W), BF16),
                   jax.ShapeDtypeStruct((b, 2, t // tm, nct * 8, B_W), F32),
                   jax.ShapeDtypeStruct((b, t, B_W), F32),
                   jax.ShapeDtypeStruct((b, t, B_W), F32)),
        grid=(b, t // tm),
        in_specs=[tokc(0), tokc(1), tokc(2), tok(OFF_GD - OFF_WD), tok(GATE_LORA),
                  full((1, B_W)), full((2, B_W)), full((2, 2 * DECAY_LORA, B_W)),
                  full((2, B_W)), full((2, 2 * AAA_LORA, B_W)), full((2, B_W)), full((1, B_W)),
                  full((GATE_LORA, B_W)), full((B_W, B_W)),
                  full((2, tm, tm)), full((2, nct * 8, tm))],
        out_specs=(dir_tok, dir_tok, dir_tok, dir_tok, tok(B_W),
                   pl.BlockSpec((1, 2, 1, nct * 8, B_W), lambda bi, i: (bi, 0, i, 0, 0)),
                   tok(B_W), tok(B_W)),
        compiler_params=_cparams(("parallel", "parallel")),
        name="prep",
    )(kvr, kvr, kvr, lora, gd, p["k_k"], p["decay_w0"], p["wdec"], p["iclr_a0"], p["wicl"],
      p["k_a"], p["r_k"], p["wgate"], p["bd"], dmat, smat)
    at, bt, kt, rt, vo, vec, bv, gate = outs
    vec = vec.reshape(b, 2, t // chunk, 8, B_W)
    return at, bt, kt, rt, vo, vec, bv, gate


def _rwkv_kernel(at0, bt0, kt0, rt0, v0, vec0, at1, bt1, kt1, rt1, v1, vec1, s0_ref,
                 y0_ref, y1_ref, sfin_ref, z_ref, *, chunk, nb):
    i = pl.program_id(1)
    L = chunk
    assert L & (L - 1) == 0
    a_refs, b_refs, k_refs, r_refs = (at0, at1), (bt0, bt1), (kt0, kt1), (rt0, rt1)
    v_refs, vec_refs, y_refs = (v0, v1), (vec0, vec1), (y0_ref, y1_ref)

    @pl.when(i == 0)
    def _():
        z_ref[...] = s0_ref[...]

    rowi = lax.broadcasted_iota(jnp.int32, (L, L), 0)
    coli = lax.broadcasted_iota(jnp.int32, (L, L), 1)
    strict = (rowi > coli, rowi < coli)
    incl = (rowi >= coli, rowi <= coli)
    eye_f = (rowi == coli).astype(F32)
    blk_xor = rowi ^ coli
    lane = lax.broadcasted_iota(jnp.int32, (L, LANES), 1)
    head0 = lane < B_HEAD
    sub = lax.broadcasted_iota(jnp.int32, (LANES, LANES), 0)
    lan = lax.broadcasted_iota(jnp.int32, (LANES, LANES), 1)
    same_head = (sub < B_HEAD) == (lan < B_HEAD)
    eye_c = sub == lan

    def split_heads(x):
        zero = jnp.zeros_like(x)
        return jnp.where(head0, x, zero), jnp.where(head0, zero, x)

    def stack_heads(x):
        x0, x1 = split_heads(x)
        return jnp.concatenate([x0, x1], axis=0)

    sls = [slice(p * LANES, (p + 1) * LANES) for p in range(N_PAIRS)]
    dps = [(n, d, p) for n in range(nb) for d in (0, 1) for p in range(N_PAIRS)]
    a_t = {(n, d, p): a_refs[d][n, 0, :, sls[p]] for n, d, p in dps}
    b_t = {(n, d, p): b_refs[d][n, 0, :, sls[p]] for n, d, p in dps}
    k_t = {(n, d, p): k_refs[d][n, 0, :, sls[p]] for n, d, p in dps}
    r_t = {(n, d, p): r_refs[d][n, 0, :, sls[p]] for n, d, p in dps}
    vv = {(n, d, p): v_refs[d][n, :, sls[p]] for n, d, p in dps}
    em = {(n, d, p): vec_refs[d][n, 0, 0, 0:1, sls[p]] for n, d, p in dps}
    pl_row = {(n, d, p): vec_refs[d][n, 0, 0, 1:2, sls[p]] for n, d, p in dps}
    epl = {(n, d, p): vec_refs[d][n, 0, 0, 2:3, sls[p]] for n, d, p in dps}
    z0 = {dp: z_ref[dp] for dp in dps}
    z0b = {dp: z0[dp].astype(BF16) for dp in dps}
    a_h = {dp: split_heads(a_t[dp]) for dp in dps}
    r_h = {dp: split_heads(r_t[dp]) for dp in dps}

    heads = [dp + (h,) for dp in dps for h in range(2)]
    tt, mf, mak, arb, ark = {}, {}, {}, {}, {}
    for ph in heads:
        dp, d, h = ph[:3], ph[1], ph[3]
        bk = jnp.concatenate([b_t[dp], k_t[dp]], axis=0)
        res = _dot_nt(jnp.concatenate([a_h[dp][h], r_h[dp][h]], axis=0), bk)
        mab = jnp.where(strict[d], res[0:L, 0:L], 0.0)
        mak[ph] = jnp.where(strict[d], res[0:L, L:2 * L], 0.0).astype(BF16)
        arb[ph] = jnp.where(incl[d], res[L:2 * L, 0:L], 0.0).astype(BF16)
        ark[ph] = jnp.where(incl[d], res[L:2 * L, L:2 * L], 0.0).astype(BF16)
        tt[ph] = eye_f + jnp.where(blk_xor == 1, mab, 0.0)
        mf[ph] = mab
    def take(x, s, d):
        if s % SUBLANES:
            return x
        first = 1 - d
        return jnp.concatenate([x[(2 * j + first) * s:(2 * j + first + 1) * s] for j in range(L // (2 * s))],
                               axis=0)

    def put(x, s, d):
        if s % SUBLANES:
            return x
        zero = jnp.zeros((s, x.shape[1]), x.dtype)
        pieces = []
        for j in range(L // (2 * s)):
            blk = x[j * s:(j + 1) * s]
            pieces += [zero, blk] if d == 0 else [blk, zero]
        return jnp.concatenate(pieces, axis=0)

    s = 2
    while s < L:
        level = (blk_xor >= s) & (blk_xor < 2 * s)
        xx = {ph: _dot(take(mf[ph], s, ph[1]).astype(BF16), tt[ph].astype(BF16)) for ph in heads}
        for ph in heads:
            w = _dot(take(tt[ph], s, ph[1]).astype(BF16), put(xx[ph], s, ph[1]).astype(BF16))
            tt[ph] = tt[ph] + jnp.where(level, put(w, s, ph[1]), 0.0)
        s *= 2

    v_stack = {dp: stack_heads(vv[dp]) for dp in dps}
    mv = {dp: _dot(jnp.concatenate([mak[dp + (0,)], mak[dp + (1,)]], axis=1), v_stack[dp]) for dp in dps}
    wu = {}
    for dp in dps:
        x_rhs = jnp.concatenate([jnp.concatenate(list(a_h[dp]), axis=0),
                                 stack_heads(mv[dp].astype(BF16))], axis=1)
        t_cat = jnp.concatenate([tt[dp + (0,)].astype(BF16), tt[dp + (1,)].astype(BF16)], axis=1)
        wu[dp] = _dot(t_cat, x_rhs)
    u = {dp: _dot((wu[dp][:, 0:LANES] * em[dp]).astype(BF16), z0b[dp]) + wu[dp][:, LANES:2 * LANES]
         for dp in dps}
    ub = {dp: u[dp].astype(BF16) for dp in dps}
    for dp in dps:
        n, d, p = dp
        y = (_dot((r_t[dp].astype(F32) * em[dp]).astype(BF16), z0b[dp])
             + _dot(jnp.concatenate([arb[dp + (0,)], arb[dp + (1,)]], axis=1), stack_heads(ub[dp]))
             + _dot(jnp.concatenate([ark[dp + (0,)], ark[dp + (1,)]], axis=1), v_stack[dp]))
        y_refs[d][n, :, sls[p]] = y
    for dp in dps:
        bhat_t = jnp.transpose(b_t[dp].astype(F32) * epl[dp]).astype(BF16)
        khat_t = jnp.transpose(k_t[dp].astype(F32) * epl[dp]).astype(BF16)
        upd = _dot(jnp.concatenate([bhat_t, khat_t], axis=1), jnp.concatenate([ub[dp], vv[dp]], axis=0))
        pl_col = jnp.sum(jnp.where(eye_c, pl_row[dp], 0.0), axis=1, keepdims=True)
        z_ref[dp] = pl_col * z0[dp] + jnp.where(same_head, upd, 0.0)

    sfin_ref[...] = z_ref[...]


def _rwkv(at, bt, kt, rt, vo, vec, s0, chunk):
    b, _, t, _ = at.shape
    nc = t // chunk
    nb = RW_NB if b % RW_NB == 0 else 1
    fwd_tok = pl.BlockSpec((nb, 1, chunk, B_W), lambda bi, i: (bi, 0, i, 0))
    rev_tok = pl.BlockSpec((nb, 1, chunk, B_W), lambda bi, i: (bi, 1, nc - 1 - i, 0))
    fwd_v = pl.BlockSpec((nb, chunk, B_W), lambda bi, i: (bi, i, 0))
    rev_v = pl.BlockSpec((nb, chunk, B_W), lambda bi, i: (bi, nc - 1 - i, 0))
    fwd_vec = pl.BlockSpec((nb, 1, 1, 8, B_W), lambda bi, i: (bi, 0, i, 0, 0))
    rev_vec = pl.BlockSpec((nb, 1, 1, 8, B_W), lambda bi, i: (bi, 1, nc - 1 - i, 0, 0))
    state = pl.BlockSpec((nb, 2, N_PAIRS, LANES, LANES), lambda bi, i: (bi, 0, 0, 0, 0))
    kern = functools.partial(_rwkv_kernel, chunk=chunk, nb=nb)
    return pl.pallas_call(
        kern,
        out_shape=(jax.ShapeDtypeStruct((b, t, B_W), F32),
                   jax.ShapeDtypeStruct((b, t, B_W), F32),
                   jax.ShapeDtypeStruct((b, 2, N_PAIRS, LANES, LANES), F32)),
        grid=(b // nb, nc),
        in_specs=[fwd_tok, fwd_tok, fwd_tok, fwd_tok, fwd_v, fwd_vec,
                  rev_tok, rev_tok, rev_tok, rev_tok, rev_v, rev_vec, state],
        out_specs=(fwd_v, rev_v, state),
        scratch_shapes=[pltpu.VMEM((nb, 2, N_PAIRS, LANES, LANES), F32)],
        compiler_params=_cparams(("parallel", "arbitrary")),
        name="rwkv",
    )(at, bt, kt, rt, vo, vec, at, bt, kt, rt, vo, vec, s0)


def _post_kernel(y0_ref, y1_ref, bv_ref, gate_ref, u_ref, va_ref, x_ref, mod_ref,
                 lnxg_ref, lnxb_ref, gmg_ref, gmb_ref, wsp_ref, bsp_ref, bd_ref,
                 wout_ref, gpost_ref, gpre_ref, wr_ref, br_ref,
                 x1_ref, h2_ref, comb_ref, *, tm):
    bdm = bd_ref[...]
    inv = 1.0 / B_HEAD

    def group_norm(val, eps):
        mu = _split_dot(val, bdm) * inv
        cen = val - mu
        var = _dot((cen * cen).astype(BF16), bdm) * inv
        return cen * lax.rsqrt(var + eps)

    y = y0_ref[0] + y1_ref[0]
    yb = group_norm(y, GN_EPS) * lnxg_ref[...] + lnxb_ref[...]
    out_b = (yb + bv_ref[0]) * gate_ref[0]

    uh = jax.nn.gelu(u_ref[0].astype(F32))
    vg = jax.nn.gelu(va_ref[0].astype(F32))
    vn = (group_norm(vg, LN_EPS) * gmg_ref[...] + gmb_ref[...]).astype(BF16)
    lane = lax.broadcasted_iota(jnp.int32, (GM_CHUNK, LANES), 1)
    head0 = lane < B_HEAD
    s_rows = []
    for c in range(tm // GM_CHUNK):
        cols = []
        for p in range(A_W // LANES):
            blk = vn[c * GM_CHUNK:(c + 1) * GM_CHUNK, p * LANES:(p + 1) * LANES]
            zero = jnp.zeros_like(blk)
            stack = jnp.concatenate([jnp.where(head0, blk, zero), jnp.where(head0, zero, blk)], axis=0)
            cols.append(_dot(wsp_ref[p], stack))
        s_rows.append(jnp.concatenate(cols, axis=1) + bsp_ref[...])
    s = jnp.concatenate(s_rows, axis=0)
    out_a = uh * s

    ymix = (_dot(out_a.astype(BF16), wout_ref[0:A_W, :]) + _dot(out_b.astype(BF16), wout_ref[A_W:, :]))
    gate1 = mod_ref[0, 0:1, :]
    shift2 = mod_ref[0, 1:2, :]
    scale2 = mod_ref[0, 2:3, :]
    ms = jnp.mean(ymix * ymix, axis=-1, keepdims=True)
    x1 = x_ref[0] + gate1 * (ymix * lax.rsqrt(ms + NORM_EPS) * gpost_ref[...])
    x1_ref[0] = x1
    ms2 = jnp.mean(x1 * x1, axis=-1, keepdims=True)
    h2 = x1 * lax.rsqrt(ms2 + NORM_EPS) * gpre_ref[...]
    h2 = h2 * (1.0 + scale2) + shift2
    for c in range(ROW_CHUNKS):
        h2_ref[0, :, c, :] = h2[:, c * LANES:(c + 1) * LANES]

    h2_hi = h2.astype(BF16)
    h2_lo = (h2 - h2_hi.astype(F32)).astype(BF16)
    logits = (_dot(h2_hi, wr_ref[0]) + _dot(h2_lo, wr_ref[0]) + _dot(h2_hi, wr_ref[1])) + br_ref[...]
    lni = lax.broadcasted_iota(jnp.int32, logits.shape, 1)
    ln = lni.astype(F32)
    lgrp = (lni // EXPERTS_PER_GROUP).astype(F32)
    neg = jnp.float32(-jnp.inf)
    big = jnp.float32(LANES)
    gmask = (lni >= N_EXPERTS) & (lni < N_EXPERTS + N_GROUPS)
    gl = jnp.where(gmask, logits, neg)
    gmax = jnp.max(gl, axis=-1, keepdims=True)
    gsum = jnp.sum(jnp.where(gmask, jnp.exp(gl - gmax), 0.0), axis=-1, keepdims=True)
    g_w = 1.0 / gsum
    g_i = jnp.min(jnp.where(gl == gmax, ln - N_EXPERTS, big), axis=-1, keepdims=True)
    emask = (lni < N_EXPERTS) & (lgrp == g_i)
    el = jnp.where(emask, logits, neg)
    l1 = jnp.max(el, axis=-1, keepdims=True)
    i1 = jnp.min(jnp.where(el == l1, ln, big), axis=-1, keepdims=True)
    el2 = jnp.where(ln == i1, neg, el)
    l2 = jnp.max(el2, axis=-1, keepdims=True)
    i2 = jnp.min(jnp.where(el2 == l2, ln, big), axis=-1, keepdims=True)
    e21 = jnp.exp(l2 - l1)
    w1 = 1.0 / (1.0 + e21)
    w2 = e21 * w1
    comb_ref[0] = (g_w * (jnp.where(ln == i1, w1, 0.0) + jnp.where(ln == i2, w2, 0.0))
                   + jnp.where((ln == i1 + MEMB) | (ln == i2 + MEMB), 1.0, 0.0))


def _post(y0, y1, bv, gate, u, va, x, mod3, p, tm):
    b, t, _ = x.shape
    tok = lambda width: pl.BlockSpec((1, tm, width), lambda bi, i: (bi, i, 0))
    full = lambda shape: pl.BlockSpec(shape, lambda bi, i: (0,) * len(shape))
    kern = functools.partial(_post_kernel, tm=tm)
    return pl.pallas_call(
        kern,
        out_shape=(jax.ShapeDtypeStruct((b, t, D_MODEL), F32),
                   jax.ShapeDtypeStruct((b, t, ROW_CHUNKS, LANES), F32),
                   jax.ShapeDtypeStruct((b, t, LANES), F32)),
        grid=(b, t // tm),
        in_specs=[tok(B_W), tok(B_W), tok(B_W), tok(B_W), tok(A_W), tok(A_W), tok(D_MODEL),
                  pl.BlockSpec((1, 8, D_MODEL), lambda bi, i: (bi, 0, 0)),
                  full((1, B_W)), full((1, B_W)), full((1, A_W)), full((1, A_W)),
                  full((A_W // LANES, GM_CHUNK, 2 * GM_CHUNK)), full((GM_CHUNK, A_W)), full((B_W, B_W)),
                  full((D_MODEL, D_MODEL)), full((1, D_MODEL)), full((1, D_MODEL)),
                  full((2, D_MODEL, LANES)), full((1, LANES))],
        out_specs=(tok(D_MODEL), pl.BlockSpec((1, tm, ROW_CHUNKS, LANES), lambda bi, i: (bi, i, 0, 0)),
                   tok(LANES)),
        compiler_params=_cparams(("parallel", "parallel")),
        name="post",
    )(y0, y1, bv, gate, u, va, x, mod3, p["lnx_g"], p["lnx_b"], p["gm_ln_g"], p["gm_ln_b"],
      p["wsp"], p["bsp"], p["bd"], p["w_out"], p["g_post1"], p["g_pre2"], p["w_router"], p["b_router"])


def _split3(x):
    hi = x.astype(BF16)
    r1 = x - hi.astype(F32)
    mid = r1.astype(BF16)
    lo = (r1 - mid.astype(F32)).astype(BF16)
    return hi, mid, lo


def _route_kernel(comb_ref, tri_ref, upper_ref, dst_ref, meta_ref, cnt_ref, carry_ref, offs_ref, ends_ref,
                  *, tm, tr, ntp):
    ph = pl.program_id(0)
    i = pl.program_id(1)
    last = pl.num_programs(1) - 1
    comb = comb_ref[...]
    lni = lax.broadcasted_iota(jnp.int32, (tm, LANES), 1)
    member = jnp.where((lni >= MEMB) & (lni < MEMB + N_EXPERTS), comb, 0.0)
    colsum = jnp.sum(member, axis=0, keepdims=True)

    @pl.when((ph == 0) & (i == 0))
    def _():
        cnt_ref[...] = jnp.zeros_like(cnt_ref)

    @pl.when(ph == 0)
    def _():
        cnt_ref[...] += colsum

    @pl.when((ph == 1) & (i == 0))
    def _():
        cnt = cnt_ref[...]
        padded = jnp.floor((cnt + (tr - 1)) * (1.0 / tr)) * tr
        hi, mid, lo = _split3(padded)
        up = upper_ref[...]
        ends = _dot(hi, up) + _dot(mid, up) + _dot(lo, up)
        ends_ref[...] = ends
        offs_ref[...] = ends - padded
        carry_ref[...] = jnp.zeros_like(carry_ref)

    @pl.when(ph == 1)
    def _():
        cum = _dot(tri_ref[...], member.astype(BF16))
        pos = offs_ref[0:1, :] + carry_ref[0:1, :] + cum
        lnf = lni.astype(F32)
        ea = jnp.min(jnp.where(member > 0, lnf, 2.0 * LANES), axis=-1, keepdims=True)
        eb = jnp.max(jnp.where(member > 0, lnf, -1.0), axis=-1, keepdims=True)
        ones8 = jnp.ones((8, LANES), BF16)
        for slot, e_sel in enumerate((ea, eb)):
            hi, mid, lo = _split3(jnp.where(lnf == e_sel, pos, 0.0))
            row = _dot_nt(ones8, hi) + _dot_nt(ones8, mid) + _dot_nt(ones8, lo)
            dst_ref[0, :, slot * tm:(slot + 1) * tm] = row.astype(jnp.int32)
        carry_ref[...] += colsum

    @pl.when((ph == 1) & (i == last))
    def _():
        ends = ends_ref[...]
        r = lax.broadcasted_iota(jnp.int32, (LANES, LANES), 0)
        c = lax.broadcasted_iota(jnp.int32, (LANES, LANES), 1)
        ends_col = jnp.sum(jnp.where(r == c, ends[0:1, :], 0.0), axis=1, keepdims=True)
        sub = lax.broadcasted_iota(jnp.int32, (LANES, ntp), 0)
        start = lax.broadcasted_iota(jnp.int32, (LANES, ntp), 1).astype(F32) * float(tr)
        is_exp = (sub >= MEMB) & (sub < MEMB + N_EXPERTS)
        done = jnp.where(is_exp & (ends_col <= start), 1.0, 0.0)
        tile_expert = jnp.minimum(jnp.sum(done, axis=0, keepdims=True), N_EXPERTS - 1.0)
        n_used = jnp.max(ends[0:1, :], axis=-1, keepdims=True) * (1.0 / tr)
        rowi = lax.broadcasted_iota(jnp.int32, (8, ntp), 0)
        meta_ref[...] = jnp.where(rowi == 0, tile_expert, n_used).astype(jnp.int32)


def _route(comb2d, tm, tr, ntp):
    n = comb2d.shape[0]
    nt = n // tm
    t = np.arange(tm)
    tri = jnp.asarray(t[None, :] < t[:, None], BF16)
    e = np.arange(LANES)
    upper = jnp.asarray(e[:, None] <= e[None, :], BF16)
    kern = functools.partial(_route_kernel, tm=tm, tr=tr, ntp=ntp)
    return pl.pallas_call(
        kern,
        out_shape=(jax.ShapeDtypeStruct((nt, 8, 2 * tm), jnp.int32),
                   jax.ShapeDtypeStruct((8, ntp), jnp.int32)),
        grid=(2, nt),
        in_specs=[pl.BlockSpec((tm, LANES), lambda ph, i: (i, 0)),
                  pl.BlockSpec((tm, tm), lambda ph, i: (0, 0)),
                  pl.BlockSpec((LANES, LANES), lambda ph, i: (0, 0))],
        out_specs=(pl.BlockSpec((1, 8, 2 * tm), lambda ph, i: (i * ph, 0, 0)),
                   pl.BlockSpec((8, ntp), lambda ph, i: (0, 0))),
        scratch_shapes=[pltpu.VMEM((8, LANES), F32)] * 4,
        compiler_params=_cparams(("arbitrary", "arbitrary")),
        name="route",
    )(comb2d, tri, upper)


DMA_UNROLL = 8


def _dispatch_kernel(dst_ref, h_ref, xs_in, xs_hbm, sem, *, tm):
    del xs_in

    def issue(blk, carry):
        for k in range(DMA_UNROLL):
            t = blk * DMA_UNROLL + k
            pltpu.make_async_copy(h_ref.at[t], xs_hbm.at[dst_ref[0, 0, t]], sem.at[0]).start()
            pltpu.make_async_copy(h_ref.at[t], xs_hbm.at[dst_ref[0, 0, tm + t]], sem.at[1]).start()
        return carry

    lax.fori_loop(0, tm // DMA_UNROLL, issue, 0)
    pltpu.make_async_copy(h_ref, xs_hbm.at[pl.ds(0, tm)], sem.at[0]).wait()
    pltpu.make_async_copy(h_ref, xs_hbm.at[pl.ds(0, tm)], sem.at[1]).wait()


def _dispatch(dst, h3, n_rows, tm):
    n = h3.shape[0]
    xs0 = jnp.zeros((n_rows, ROW_CHUNKS, LANES), F32)
    kern = functools.partial(_dispatch_kernel, tm=tm)
    return pl.pallas_call(
        kern,
        out_shape=jax.ShapeDtypeStruct((n_rows, ROW_CHUNKS, LANES), F32),
        grid=(n // tm,),
        in_specs=[pl.BlockSpec((1, 1, 2 * tm), lambda i: (i, 0, 0), memory_space=pltpu.SMEM),
                  pl.BlockSpec((tm, ROW_CHUNKS, LANES), lambda i: (i, 0, 0)),
                  pl.BlockSpec(memory_space=pl.ANY)],
        out_specs=pl.BlockSpec(memory_space=pl.ANY),
        scratch_shapes=[pltpu.SemaphoreType.DMA((2,))],
        input_output_aliases={2: 0},
        compiler_params=_cparams(("arbitrary",)),
        name="dispatch",
    )(dst, h3, xs0)


def _gmm_kernel(meta_ref, xs_ref, wg_ref, wu_ref, wd_ref, ys_ref):
    j = pl.program_id(0)
    n_used = meta_ref[1, 0]

    @pl.when(j < n_used)
    def _():
        x = jnp.concatenate([xs_ref[:, c, :] for c in range(ROW_CHUNKS)], axis=1).astype(BF16)
        gate = _dot(x, wg_ref[0])
        up = _dot(x, wu_ref[0])
        hid = (gate * jax.nn.sigmoid(gate) * up).astype(BF16)
        y = _dot(hid, wd_ref[0])
        for c in range(ROW_CHUNKS):
            ys_ref[:, c, :] = y[:, c * LANES:(c + 1) * LANES]

    @pl.when(j >= n_used)
    def _():
        ys_ref[...] = jnp.zeros_like(ys_ref)


def _gmm(meta, xs, wg, wu, wd, tr):
    n_rows = xs.shape[0]
    row_spec = lambda fn: pl.BlockSpec((tr, ROW_CHUNKS, LANES), fn)
    return pl.pallas_call(
        _gmm_kernel,
        out_shape=jax.ShapeDtypeStruct(xs.shape, F32),
        grid_spec=pltpu.PrefetchScalarGridSpec(
            num_scalar_prefetch=1,
            grid=(n_rows // tr,),
            in_specs=[row_spec(lambda j, m: (jnp.minimum(j, m[1, 0] - 1), 0, 0)),
                      pl.BlockSpec((1, D_MODEL, D_EXPERT), lambda j, m: (m[0, j], 0, 0)),
                      pl.BlockSpec((1, D_MODEL, D_EXPERT), lambda j, m: (m[0, j], 0, 0)),
                      pl.BlockSpec((1, D_EXPERT, D_MODEL), lambda j, m: (m[0, j], 0, 0))],
            out_specs=row_spec(lambda j, m: (j, 0, 0))),
        compiler_params=_cparams(("arbitrary",)),
        name="gmm",
    )(meta, xs, wg, wu, wd)


def _combine_kernel(dst_ref, ys_hbm, comb_ref, x1_ref, mod_ref, gpost_ref, o_ref, buf_a, buf_b, sem, *, tm):
    def issue(blk, carry):
        for k in range(DMA_UNROLL):
            t = blk * DMA_UNROLL + k
            pltpu.make_async_copy(ys_hbm.at[dst_ref[0, 0, t]], buf_a.at[t], sem.at[0]).start()
            pltpu.make_async_copy(ys_hbm.at[dst_ref[0, 0, tm + t]], buf_b.at[t], sem.at[1]).start()
        return carry

    lax.fori_loop(0, tm // DMA_UNROLL, issue, 0)
    pltpu.make_async_copy(ys_hbm.at[pl.ds(0, tm)], buf_a, sem.at[0]).wait()
    pltpu.make_async_copy(ys_hbm.at[pl.ds(0, tm)], buf_b, sem.at[1]).wait()

    comb = comb_ref[0]
    lni = lax.broadcasted_iota(jnp.int32, comb.shape, 1)
    lnf = lni.astype(F32)
    member = (lni >= MEMB) & (lni < MEMB + N_EXPERTS) & (comb > 0)
    ea = jnp.min(jnp.where(member, lnf, 2.0 * LANES), axis=-1, keepdims=True) - MEMB
    eb = jnp.max(jnp.where(member, lnf, -1.0), axis=-1, keepdims=True) - MEMB
    wa = jnp.sum(jnp.where(lnf == ea, comb, 0.0), axis=-1, keepdims=True)
    wb = jnp.sum(jnp.where(lnf == eb, comb, 0.0), axis=-1, keepdims=True)
    ya = jnp.concatenate([buf_a[:, c, :] for c in range(ROW_CHUNKS)], axis=1)
    yb = jnp.concatenate([buf_b[:, c, :] for c in range(ROW_CHUNKS)], axis=1)
    y = wa * ya + wb * yb
    ms = jnp.mean(y * y, axis=-1, keepdims=True)
    o_ref[0] = x1_ref[0] + mod_ref[0, 3:4, :] * (y * lax.rsqrt(ms + NORM_EPS) * gpost_ref[...])


def _combine(dst, ys, comb, x1, mod3, g_post2, tm):
    b, t, _ = x1.shape
    nti = t // tm
    tok = lambda width: pl.BlockSpec((1, tm, width), lambda bi, i: (bi, i, 0))
    kern = functools.partial(_combine_kernel, tm=tm)
    return pl.pallas_call(
        kern,
        out_shape=jax.ShapeDtypeStruct((b, t, D_MODEL), F32),
        grid=(b, nti),
        in_specs=[pl.BlockSpec((1, 1, 2 * tm), lambda bi, i: (bi * nti + i, 0, 0), memory_space=pltpu.SMEM),
                  pl.BlockSpec(memory_space=pl.ANY),
                  tok(LANES), tok(D_MODEL),
                  pl.BlockSpec((1, 8, D_MODEL), lambda bi, i: (bi, 0, 0)),
                  pl.BlockSpec((1, D_MODEL), lambda bi, i: (0, 0))],
        out_specs=tok(D_MODEL),
        scratch_shapes=[pltpu.VMEM((tm, ROW_CHUNKS, LANES), F32), pltpu.VMEM((tm, ROW_CHUNKS, LANES), F32),
                        pltpu.SemaphoreType.DMA((2,))],
        compiler_params=_cparams(("arbitrary", "arbitrary")),
        name="combine",
    )(dst, ys, comb, x1, mod3, g_post2.reshape(1, D_MODEL))


def _moe(h3, comb, x1, mod3, wg, wu, wd, g_post2):
    b, t, _ = x1.shape
    n = b * t
    tm = _tile(t, MOE_TM)
    n_rows = 2 * n + N_EXPERTS * MOE_TR
    ntp = -(-(n_rows // MOE_TR) // LANES) * LANES
    dst8, meta = _route(comb.reshape(n, LANES), tm, MOE_TR, ntp)
    dst = dst8[:, 0:1, :]
    xs = _dispatch(dst, h3.reshape(n, ROW_CHUNKS, LANES), n_rows, tm)
    ys = _gmm(meta[0:2], xs, wg, wu, wd, MOE_TR)
    return _combine(dst, ys, comb, x1, mod3, g_post2, tm)


def _tile(t, pref):
    while t % pref:
        pref //= 2
    return pref


def kernel(x, c, ctx, c_ctx, w_mod, b_mod, g_pre1, g_post1, g_pre2, g_post2, w_in, conv_rkv, gm_ln_g, gm_ln_b, w_spatial, b_spatial, decay_w0, decay_up, iclr_a0, iclr_up, k_k, k_a, r_k, outgate_up, lnx_g, lnx_b, w_out, w_router_grp, b_router_grp, w_router_exp, b_router_exp, w_gate, w_up, w_down):
    assert w_mod.shape[0] == 1, "single-layer kernel"
    b, t, _ = x.shape
    tc = ctx.shape[1]
    assert b <= 7 and t % RW_CHUNK == 0 and tc % RW_CHUNK == 0 and t % GRID_W == 0

    head_of = np.arange(B_W) // B_HEAD
    bd = jnp.asarray(head_of[:, None] == head_of[None, :], BF16)
    zpad = jnp.zeros((2, DECAY_LORA, B_W), F32)
    wdec = jnp.stack([jnp.concatenate([decay_up[0, 0], zpad[0]], 0),
                      jnp.concatenate([zpad[0], decay_up[0, 1]], 0)]).astype(BF16)
    wicl = jnp.stack([jnp.concatenate([iclr_up[0, 0], zpad[0]], 0),
                      jnp.concatenate([zpad[0], iclr_up[0, 1]], 0)]).astype(BF16)
    wsp = w_spatial[0].reshape(A_W // LANES, 2, GM_CHUNK, GM_CHUNK)
    wsp = jnp.concatenate([wsp[:, 0], wsp[:, 1]], axis=-1).astype(BF16)
    bsp = jnp.repeat(b_spatial[0].T, A_W // A_GROUPS, axis=1)
    w_router = jnp.concatenate(
        [w_router_exp[0], w_router_grp[0],
         jnp.zeros((D_MODEL, LANES - N_EXPERTS - N_GROUPS), F32)], axis=1)
    w_router_hi = w_router.astype(BF16)
    w_router = jnp.stack([w_router_hi, (w_router - w_router_hi.astype(F32)).astype(BF16)])
    b_router = jnp.concatenate(
        [b_router_exp[0], b_router_grp[0], jnp.zeros((LANES - N_EXPERTS - N_GROUPS,), F32)]).reshape(1, LANES)
    prm = {
        "k_k": k_k[0].reshape(1, B_W), "decay_w0": decay_w0[0], "wdec": wdec,
        "iclr_a0": iclr_a0[0], "wicl": wicl, "k_a": k_a[0], "r_k": r_k[0].reshape(1, B_W),
        "wgate": outgate_up[0].astype(BF16), "bd": bd,
        "lnx_g": lnx_g[0].reshape(1, B_W), "lnx_b": lnx_b[0].reshape(1, B_W),
        "gm_ln_g": gm_ln_g[0].reshape(1, A_W), "gm_ln_b": gm_ln_b[0].reshape(1, A_W),
        "wsp": wsp, "bsp": bsp, "w_out": w_out[0].astype(BF16),
        "g_post1": g_post1[0].reshape(1, D_MODEL), "g_pre2": g_pre2[0].reshape(1, D_MODEL),
        "w_router": w_router, "b_router": b_router,
    }
    w_in_b = w_in[0].astype(BF16)

    cond8 = jnp.concatenate([c, c_ctx[None, :], jnp.zeros((8 - b - 1, D_MODEL), F32)], axis=0)
    mod = _mod(cond8, w_mod[0], b_mod[0])
    mod6 = mod.reshape(8, 6, D_MODEL)
    shift1, scale1 = mod6[:b, 0:1], mod6[:b, 1:2]
    cshift1 = jnp.broadcast_to(mod6[b:b + 1, 0:1], (b, 1, D_MODEL))
    cscale1 = jnp.broadcast_to(mod6[b:b + 1, 1:2], (b, 1, D_MODEL))
    mod3 = jnp.concatenate([mod6[:b, 2:6], jnp.zeros((b, 4, D_MODEL), F32)], axis=1)

    c_rkv, c_lora, c_gd, _, _ = _inproj(ctx, g_pre1[0], cshift1, cscale1, w_in_b, _tile(tc, 256))
    c_kvr = _conv(c_rkv, conv_rkv[0], tc, _tile(tc, 256), 3 * B_W)
    c_at, c_bt, c_kt, c_rt, c_v, c_vec, _, _ = _prep(c_kvr, c_lora, c_gd, prm, _tile(tc, 256), RW_CHUNK)
    s_zero = jnp.zeros((b, 2, N_PAIRS, LANES, LANES), F32)
    _, _, states = _rwkv(c_at, c_bt, c_kt, c_rt, c_v, c_vec, s_zero, RW_CHUNK)

    rkv, lora, gd, u, va = _inproj(x, g_pre1[0], shift1, scale1, w_in_b, _tile(t, 512))
    kvr = _conv(rkv, conv_rkv[0], GRID_W, _tile(t, 512), 3 * B_W)
    at, bt, kt, rt, vo, vec, bv, gate = _prep(kvr, lora, gd, prm, _tile(t, 256), RW_CHUNK)
    y0, y1, _ = _rwkv(at, bt, kt, rt, vo, vec, states, RW_CHUNK)
    x1, h2, comb = _post(y0, y1, bv, gate, u, va, x, mod3, prm, _tile(t, 512))

    return _moe(h2, comb, x1, mod3, w_gate[0].astype(BF16), w_up[0].astype(BF16), w_down[0].astype(BF16),
                g_post2[0])
```
